```python
import numpy as np
import jax
import jax.numpy as jnp
from jax import lax

D_MODEL = 2048
BATCH = 4
SEQ = 4096
DEPTH = 1

NSA_HEADS = 16
NSA_KV_HEADS = 4
NSA_GROUP = NSA_HEADS // NSA_KV_HEADS
NSA_HEAD_DIM = 64
NSA_Q_WIDTH = NSA_HEADS * NSA_HEAD_DIM
NSA_KV_WIDTH = NSA_KV_HEADS * NSA_HEAD_DIM
CMP_BLOCK = 32
CMP_STRIDE = 16
CMP_HIDDEN = 256
SLC_BLOCK = 64
SLC_TOPN = 16
WINDOW = 512
Q_BLOCK = 64

M_HEADS = 4
M_HEAD_DIM = 256
M_WIDTH = M_HEADS * M_HEAD_DIM
M_CHUNK = 64
CONV_WIDTH = 4

D_FF = 5632

EPS = 1e-6
NEG_INF = -1e30
FORCE_SCORE = 1e4

SPLIT_SIZES = (NSA_Q_WIDTH, 6 * NSA_KV_WIDTH, 3 * NSA_HEADS, 3 * M_WIDTH, M_HEADS, M_HEADS, M_WIDTH, 2 * D_MODEL)
IN_WIDTH = NSA_Q_WIDTH + 6 * NSA_KV_WIDTH + 3 * NSA_HEADS + 3 * M_WIDTH + 2 * M_HEADS + M_WIDTH + 2 * D_MODEL
F_GATE_START = NSA_Q_WIDTH + 6 * NSA_KV_WIDTH + 3 * NSA_HEADS + 3 * M_WIDTH + M_HEADS

kernel_name = "hybrid_nsa_mlstm_macaron_block"


def rms_norm(x, gain):
    x32 = x.astype(jnp.float32)
    y = x32 * lax.rsqrt(jnp.mean(x32 * x32, axis=-1, keepdims=True) + EPS)
    return (y * gain.astype(jnp.float32)).astype(x.dtype)


def swiglu(x, w_gate, w_up, w_down):
    return (jax.nn.silu(x @ w_gate) * (x @ w_up)) @ w_down


def masked_softmax(s, mask):
    s = jnp.where(mask, s.astype(jnp.float32), NEG_INF)
    return jax.nn.softmax(s, axis=-1) * mask


def split_columns(p):
    out, start = [], 0
    for size in SPLIT_SIZES:
        out.append(p[..., start:start + size])
        start += size
    return out


def causal_depthwise_conv(u, w, bias):
    c = u.shape[-1]
    out = lax.conv_general_dilated(u, w[:, None, :].astype(u.dtype), window_strides=(1,),
                                   padding=[(CONV_WIDTH - 1, 0)],
                                   dimension_numbers=("NWC", "WIO", "NWC"),
                                   feature_group_count=c)
    return out + bias


def compress_blocks(kv, pos_emb, w1, w2):
    b, g, s, d = kv.shape
    n_cmp = (s - CMP_BLOCK) // CMP_STRIDE + 1
    idx = np.arange(n_cmp)[:, None] * CMP_STRIDE + np.arange(CMP_BLOCK)[None, :]
    blocks = kv[:, :, idx, :] + pos_emb
    flat = blocks.reshape(b, g, n_cmp, CMP_BLOCK * d)
    return jax.nn.gelu(flat @ w1) @ w2


def nsa_attention(q, k_cmp, v_cmp, k_slc, v_slc, k_win, v_win, gates,
                  q_gain, kc_gain, ks_gain, kw_gain,
                  cmp_pos_k, cmp_w1_k, cmp_w2_k, cmp_pos_v, cmp_w1_v, cmp_w2_v):
    b, g, r, s, d = q.shape
    q = rms_norm(q, q_gain) * (d ** -0.5)
    t = np.arange(s)

    kc = rms_norm(compress_blocks(k_cmp, cmp_pos_k, cmp_w1_k, cmp_w2_k), kc_gain)
    vc = compress_blocks(v_cmp, cmp_pos_v, cmp_w1_v, cmp_w2_v)
    n_cmp = kc.shape[2]
    cmp_start = np.arange(n_cmp) * CMP_STRIDE
    cmp_mask = (cmp_start + CMP_BLOCK - 1)[None, :] <= t[:, None]
    p_cmp = masked_softmax(jnp.einsum("bgrsd,bgcd->bgrsc", q, kc), cmp_mask)
    o_cmp = jnp.einsum("bgrsc,bgcd->bgrsd", p_cmp.astype(vc.dtype), vc)

    n_slc = s // SLC_BLOCK
    blk = np.arange(n_slc)
    blk_start = blk * SLC_BLOCK
    overlap = ((cmp_start[:, None] < (blk_start + SLC_BLOCK)[None, :]) &
               ((cmp_start + CMP_BLOCK)[:, None] > blk_start[None, :])).astype(np.float32)
    imp = jnp.einsum("bgrsc,cn->bgsn", p_cmp, overlap)
    cur = t // SLC_BLOCK
    forced = (blk[None, :] == 0) | (blk[None, :] == cur[:, None]) | (blk[None, :] == cur[:, None] - 1)
    causal_blk = blk_start[None, :] <= t[:, None]
    imp = jnp.where(forced, FORCE_SCORE, jnp.where(causal_blk, imp, -1.0))
    n_sel = min(SLC_TOPN, n_slc)
    _, sel = lax.top_k(imp, n_sel)

    kb = rms_norm(k_slc, ks_gain).reshape(b, g, n_slc, SLC_BLOCK, d)
    vb = v_slc.reshape(b, g, n_slc, SLC_BLOCK, d)
    pad = ((0, 0), (0, 0), (WINDOW, 0), (0, 0))
    kw = jnp.pad(rms_norm(k_win, kw_gain), pad)
    vw = jnp.pad(v_win, pad)
    bi = jnp.arange(b)[:, None, None, None]
    gi = jnp.arange(g)[None, :, None, None]

    def query_block(i):
        s0 = i * Q_BLOCK
        tq = s0 + jnp.arange(Q_BLOCK)
        qi = lax.dynamic_slice_in_dim(q, s0, Q_BLOCK, axis=3)
        idx = lax.dynamic_slice_in_dim(sel, s0, Q_BLOCK, axis=2)
        ks = kb[bi, gi, idx]
        vs = vb[bi, gi, idx]
        kpos = idx[..., None] * SLC_BLOCK + jnp.arange(SLC_BLOCK)
        smask = (kpos <= tq[:, None, None]).reshape(b, g, 1, Q_BLOCK, n_sel * SLC_BLOCK)
        ss = jnp.einsum("bgrqd,bgqnld->bgrqnl", qi, ks).reshape(b, g, r, Q_BLOCK, n_sel * SLC_BLOCK)
        ps = masked_softmax(ss, smask).reshape(b, g, r, Q_BLOCK, n_sel, SLC_BLOCK)
        o_s = jnp.einsum("bgrqnl,bgqnld->bgrqd", ps.astype(vs.dtype), vs)
        kwi = lax.dynamic_slice_in_dim(kw, s0, Q_BLOCK + WINDOW, axis=2)
        vwi = lax.dynamic_slice_in_dim(vw, s0, Q_BLOCK + WINDOW, axis=2)
        wpos = s0 - WINDOW + jnp.arange(Q_BLOCK + WINDOW)
        wmask = ((wpos[None, :] <= tq[:, None]) & (wpos[None, :] > tq[:, None] - WINDOW)
                 & (wpos[None, :] >= 0))
        pw = masked_softmax(jnp.einsum("bgrqd,bgkd->bgrqk", qi, kwi), wmask)
        o_w = jnp.einsum("bgrqk,bgkd->bgrqd", pw.astype(vwi.dtype), vwi)
        return o_s, o_w

    o_slc, o_win = lax.map(query_block, jnp.arange(s // Q_BLOCK))
    o_slc = jnp.moveaxis(o_slc, 0, 3).reshape(b, g, r, s, d)
    o_win = jnp.moveaxis(o_win, 0, 3).reshape(b, g, r, s, d)

    gt = jax.nn.sigmoid(gates.astype(jnp.float32)).astype(q.dtype)
    return gt[..., 0:1] * o_cmp + gt[..., 1:2] * o_slc + gt[..., 2:3] * o_win


def mlstm_chunkwise(q, k, v, i_pre, f_pre):
    b, h, s, d = q.shape
    L = M_CHUNK
    nc = s // L
    q = q.reshape(b, h, nc, L, d)
    k = k.reshape(b, h, nc, L, d) * (d ** -0.5)
    v = v.reshape(b, h, nc, L, d)
    log_f = jax.nn.log_sigmoid(f_pre).reshape(b, h, nc, L)
    log_i = i_pre.reshape(b, h, nc, L)
    a = jnp.cumsum(log_f, axis=-1)
    g = a[..., -1]

    causal = np.tril(np.ones((L, L), dtype=bool))
    log_w = jnp.where(causal, a[..., :, None] - a[..., None, :] + log_i[..., None, :], NEG_INF)
    m_intra = jnp.max(log_w, axis=-1)
    w_intra = jnp.exp(log_w - m_intra[..., None]) * jnp.einsum("bhcjd,bhcsd->bhcjs", q, k)
    num_intra = jnp.einsum("bhcjs,bhcsd->bhcjd", w_intra, v)
    den_intra = jnp.sum(w_intra, axis=-1)

    log_u = g[..., None] - a + log_i

    def step(carry, xs):
        c_st, n_st, m_st = carry
        q_c, k_c, v_c, g_c, lu_c = xs
        num_inter = jnp.einsum("bhjd,bhde->bhje", q_c, c_st)
        den_inter = jnp.einsum("bhjd,bhd->bhj", q_c, n_st)
        m_new = jnp.maximum(g_c + m_st, jnp.max(lu_c, axis=-1))
        decay = jnp.exp(g_c + m_st - m_new)
        u = jnp.exp(lu_c - m_new[..., None])
        uk = u[..., None] * k_c
        c_new = decay[..., None, None] * c_st + jnp.einsum("bhsd,bhse->bhde", uk, v_c)
        n_new = decay[..., None] * n_st + jnp.sum(uk, axis=2)
        return (c_new, n_new, m_new), (num_inter, den_inter, m_st)

    init = (jnp.zeros((b, h, d, d), jnp.float32), jnp.zeros((b, h, d), jnp.float32),
            jnp.zeros((b, h), jnp.float32))
    xs = (jnp.moveaxis(q, 2, 0), jnp.moveaxis(k, 2, 0), jnp.moveaxis(v, 2, 0),
          jnp.moveaxis(g, 2, 0), jnp.moveaxis(log_u, 2, 0))
    _, (num_inter, den_inter, m_prev) = lax.scan(step, init, xs)
    num_inter = jnp.moveaxis(num_inter, 0, 2)
    den_inter = jnp.moveaxis(den_inter, 0, 2)
    m_prev = jnp.moveaxis(m_prev, 0, 2)

    log_inter = a + m_prev[..., None]
    m_comb = jnp.maximum(log_inter, m_intra)
    s_inter = jnp.exp(log_inter - m_comb)
    s_intra = jnp.exp(m_intra - m_comb)
    num = s_inter[..., None] * num_inter + s_intra[..., None] * num_intra
    den = s_inter * den_inter + s_intra * den_intra
    hcell = num / jnp.maximum(jnp.abs(den), jnp.exp(-m_comb))[..., None]
    return hcell.reshape(b, h, s, d)


def setup_inputs(seed: int = 0) -> dict:
    key = jax.random.key(seed)
    ks = jax.random.split(key, 40)
    L = DEPTH

    def nrm(k, shape, scale):
        return jax.random.normal(k, shape, jnp.float32) * scale

    def gain(k, shape):
        return 1.0 + 0.02 * jax.random.normal(k, shape, jnp.float32)

    b_in = nrm(ks[6], (L, IN_WIDTH), 0.02)
    b_in = b_in.at[:, F_GATE_START:F_GATE_START + M_HEADS].add(jnp.linspace(3.0, 6.0, M_HEADS))
    cmp_in = CMP_BLOCK * NSA_HEAD_DIM
    return {
        "x": nrm(ks[0], (BATCH, SEQ, D_MODEL), 1.0),
        "ffn1_norm": gain(ks[1], (L, D_MODEL)),
        "ffn1_w_gate": nrm(ks[2], (L, D_MODEL, D_FF), D_MODEL ** -0.5),
        "ffn1_w_up": nrm(ks[3], (L, D_MODEL, D_FF), D_MODEL ** -0.5),
        "ffn1_w_down": nrm(ks[4], (L, D_FF, D_MODEL), D_FF ** -0.5),
        "mix_norm": gain(ks[5], (L, D_MODEL)),
        "w_in": nrm(ks[7], (L, D_MODEL, IN_WIDTH), D_MODEL ** -0.5),
        "b_in": b_in,
        "nsa_q_gain": gain(ks[8], (L, NSA_HEAD_DIM)),
        "nsa_kc_gain": gain(ks[9], (L, NSA_HEAD_DIM)),
        "nsa_ks_gain": gain(ks[10], (L, NSA_HEAD_DIM)),
        "nsa_kw_gain": gain(ks[11], (L, NSA_HEAD_DIM)),
        "cmp_pos_k": nrm(ks[12], (L, CMP_BLOCK, NSA_HEAD_DIM), 0.1),
        "cmp_w1_k": nrm(ks[13], (L, cmp_in, CMP_HIDDEN), cmp_in ** -0.5),
        "cmp_w2_k": nrm(ks[14], (L, CMP_HIDDEN, NSA_HEAD_DIM), CMP_HIDDEN ** -0.5),
        "cmp_pos_v": nrm(ks[15], (L, CMP_BLOCK, NSA_HEAD_DIM), 0.1),
        "cmp_w1_v": nrm(ks[16], (L, cmp_in, CMP_HIDDEN), cmp_in ** -0.5),
        "cmp_w2_v": nrm(ks[17], (L, CMP_HIDDEN, NSA_HEAD_DIM), CMP_HIDDEN ** -0.5),
        "m_conv_w": nrm(ks[18], (L, CONV_WIDTH, 2 * M_WIDTH), CONV_WIDTH ** -0.5),
        "m_conv_b": nrm(ks[19], (L, 2 * M_WIDTH), 0.02),
        "m_out_gain": gain(ks[20], (L, M_HEADS, M_HEAD_DIM)),
        "w_branch_nsa": nrm(ks[21], (L, NSA_Q_WIDTH, D_MODEL), NSA_Q_WIDTH ** -0.5),
        "w_branch_mlstm": nrm(ks[22], (L, M_WIDTH, D_MODEL), M_WIDTH ** -0.5),
        "w_out": nrm(ks[23], (L, D_MODEL, D_MODEL), D_MODEL ** -0.5),
        "ffn2_norm": gain(ks[24], (L, D_MODEL)),
        "ffn2_w_gate": nrm(ks[25], (L, D_MODEL, D_FF), D_MODEL ** -0.5),
        "ffn2_w_up": nrm(ks[26], (L, D_MODEL, D_FF), D_MODEL ** -0.5),
        "ffn2_w_down": nrm(ks[27], (L, D_FF, D_MODEL), D_FF ** -0.5),
    }


def reference(x, ffn1_norm, ffn1_w_gate, ffn1_w_up, ffn1_w_down, mix_norm, w_in, b_in,
              nsa_q_gain, nsa_kc_gain, nsa_ks_gain, nsa_kw_gain,
              cmp_pos_k, cmp_w1_k, cmp_w2_k, cmp_pos_v, cmp_w1_v, cmp_w2_v,
              m_conv_w, m_conv_b, m_out_gain, w_branch_nsa, w_branch_mlstm, w_out,
              ffn2_norm, ffn2_w_gate, ffn2_w_up, ffn2_w_down):
    b, s, _ = x.shape
    G, R, dh = NSA_KV_HEADS, NSA_GROUP, NSA_HEAD_DIM
    for l in range(DEPTH):
        x = x + 0.5 * swiglu(rms_norm(x, ffn1_norm[l]), ffn1_w_gate[l], ffn1_w_up[l], ffn1_w_down[l])

        hn = rms_norm(x, mix_norm[l])
        proj = hn @ w_in[l] + b_in[l]
        p_q, p_kv, p_g, p_mqkv, p_mi, p_mf, p_mo, p_merge = split_columns(proj)

        q_n = p_q.reshape(b, s, G, R, dh).transpose(0, 2, 3, 1, 4)
        kv_n = p_kv.reshape(b, s, 6, G, dh).transpose(2, 0, 3, 1, 4)
        g_n = p_g.reshape(b, s, G, R, 3).transpose(0, 2, 3, 1, 4)
        o_nsa = nsa_attention(q_n, kv_n[0], kv_n[1], kv_n[2], kv_n[3], kv_n[4], kv_n[5], g_n,
                              nsa_q_gain[l], nsa_kc_gain[l], nsa_ks_gain[l], nsa_kw_gain[l],
                              cmp_pos_k[l], cmp_w1_k[l], cmp_w2_k[l],
                              cmp_pos_v[l], cmp_w1_v[l], cmp_w2_v[l])
        o_nsa = o_nsa.transpose(0, 3, 1, 2, 4).reshape(b, s, NSA_Q_WIDTH)

        qk_m = jax.nn.silu(causal_depthwise_conv(p_mqkv[..., :2 * M_WIDTH], m_conv_w[l], m_conv_b[l]))
        v_m = p_mqkv[..., 2 * M_WIDTH:]

        def heads(u):
            return u.reshape(b, s, M_HEADS, M_HEAD_DIM).transpose(0, 2, 1, 3).astype(jnp.float32)

        hcell = mlstm_chunkwise(heads(qk_m[..., :M_WIDTH]), heads(qk_m[..., M_WIDTH:]), heads(v_m),
                                p_mi.transpose(0, 2, 1).astype(jnp.float32),
                                p_mf.transpose(0, 2, 1).astype(jnp.float32))
        hcell = rms_norm(hcell.transpose(0, 2, 1, 3), m_out_gain[l]).reshape(b, s, M_WIDTH)
        h_m = (jax.nn.sigmoid(p_mo.astype(jnp.float32)) * hcell).astype(x.dtype)

        gates = jax.nn.sigmoid(p_merge.astype(jnp.float32)).astype(x.dtype)
        merged = gates[..., :D_MODEL] * (o_nsa @ w_branch_nsa[l]) + gates[..., D_MODEL:] * (h_m @ w_branch_mlstm[l])
        x = x + merged @ w_out[l]

        x = x + 0.5 * swiglu(rms_norm(x, ffn2_norm[l]), ffn2_w_gate[l], ffn2_w_up[l], ffn2_w_down[l])
    return x
```

```python
import functools

import numpy as np
import jax
import jax.numpy as jnp
from jax import lax
from jax.experimental import pallas as pl
from jax.experimental.pallas import tpu as pltpu

F32 = jnp.float32
BF16 = jnp.bfloat16

D_MODEL = 2048
D_FF = 5632
NSA_HEADS = 16
NSA_KV_HEADS = 4
NSA_GROUP = NSA_HEADS // NSA_KV_HEADS
NSA_HEAD_DIM = 64
NSA_Q_WIDTH = NSA_HEADS * NSA_HEAD_DIM
NSA_KV_WIDTH = NSA_KV_HEADS * NSA_HEAD_DIM
CMP_BLOCK = 32
CMP_STRIDE = 16
CMP_HIDDEN = 256
SLC_BLOCK = 64
SLC_TOPN = 16
WINDOW = 512
M_HEADS = 4
M_HEAD_DIM = 256
M_WIDTH = M_HEADS * M_HEAD_DIM
M_CHUNK = 64
CONV_WIDTH = 4
EPS = 1e-6
NEG_INF = -1e30
FORCE_SCORE = 1e4

_OFF_Q = 0
_OFF_KV = _OFF_Q + NSA_Q_WIDTH
_OFF_G = _OFF_KV + 6 * NSA_KV_WIDTH
_OFF_MQKV = _OFF_G + 3 * NSA_HEADS
_OFF_MI = _OFF_MQKV + 3 * M_WIDTH
_OFF_MF = _OFF_MI + M_HEADS
_OFF_MO = _OFF_MF + M_HEADS
_OFF_MERGE = _OFF_MO + M_WIDTH
_IN_WIDTH = _OFF_MERGE + 2 * D_MODEL

_P_MERGE = 0
_P_Q = 2 * D_MODEL
_P_MQ = _P_Q + NSA_Q_WIDTH
_P_MO = _P_MQ + 3 * M_WIDTH
_P_KV = _P_MO + M_WIDTH
_P_WIDTH = _P_KV + 6 * NSA_KV_WIDTH
_SMALL_WIDTH = 128
_S_MI = 3 * NSA_HEADS
_S_MF = _S_MI + M_HEADS

VMEM_LIMIT_BYTES = 56 * 1024 * 1024

_NT = (((1,), (1,)), ((), ()))
_TN = (((0,), (0,)), ((), ()))


def _params(*sem):
    return pltpu.CompilerParams(dimension_semantics=sem, vmem_limit_bytes=VMEM_LIMIT_BYTES)


def _dot(a, b):
    return jnp.dot(a, b, preferred_element_type=F32)


def _split3(x):
    hi = x.astype(BF16)
    r1 = x - hi.astype(F32)
    mid = r1.astype(BF16)
    lo = (r1 - mid.astype(F32)).astype(BF16)
    return hi, mid, lo


def _rms(x, gain):
    return x * lax.rsqrt(jnp.mean(x * x, axis=-1, keepdims=True) + EPS) * gain


def _ffn_kernel(*refs, n_ff, emit_mix):
    if emit_mix:
        (x_ref, g_ref, wg_ref, wu_ref, wd_ref, g2_ref, ws_ref, bs_ref,
         o_ref, hn2_ref, small_ref, hn_ref, acc_ref) = refs
    else:
        x_ref, g_ref, wg_ref, wu_ref, wd_ref, o_ref, hn_ref, acc_ref = refs
    j = pl.program_id(1)

    @pl.when(j == 0)
    def _():
        hn_ref[...] = _rms(x_ref[...], g_ref[...]).astype(BF16)
        acc_ref[...] = jnp.zeros_like(acc_ref)

    hn = hn_ref[...]
    a = _dot(hn, wg_ref[...])
    b = _dot(hn, wu_ref[...])
    h = (a * jax.nn.sigmoid(a) * b).astype(BF16)
    acc_ref[...] += _dot(h, wd_ref[...])

    @pl.when(j == n_ff - 1)
    def _():
        y = x_ref[...] + 0.5 * acc_ref[...]
        o_ref[...] = y
        if emit_mix:
            hn2 = _rms(y, g2_ref[...]).astype(BF16)
            hn2_ref[...] = hn2
            small_ref[...] = _dot(hn2, ws_ref[...]) + bs_ref[...]


def _ffn(x, gain, wg, wu, wd, mix=None, *, tm=512, tf=512):
    n, d = x.shape
    f = wg.shape[1]
    assert n % tm == 0 and f % tf == 0
    n_ff = f // tf
    in_specs = [
        pl.BlockSpec((tm, d), lambda i, j: (i, 0)),
        pl.BlockSpec((1, d), lambda i, j: (0, 0)),
        pl.BlockSpec((d, tf), lambda i, j: (0, j)),
        pl.BlockSpec((d, tf), lambda i, j: (0, j)),
        pl.BlockSpec((tf, d), lambda i, j: (j, 0)),
    ]
    out_shape = [jax.ShapeDtypeStruct((n, d), F32)]
    out_specs = [pl.BlockSpec((tm, d), lambda i, j: (i, 0))]
    args = [x, gain, wg, wu, wd]
    if mix is not None:
        g2, ws, bs = mix
        in_specs += [
            pl.BlockSpec((1, d), lambda i, j: (0, 0)),
            pl.BlockSpec((d, _SMALL_WIDTH), lambda i, j: (0, 0)),
            pl.BlockSpec((1, _SMALL_WIDTH), lambda i, j: (0, 0)),
        ]
        out_shape += [jax.ShapeDtypeStruct((n, d), BF16), jax.ShapeDtypeStruct((n, _SMALL_WIDTH), F32)]
        out_specs += [pl.BlockSpec((tm, d), lambda i, j: (i, 0)),
                      pl.BlockSpec((tm, _SMALL_WIDTH), lambda i, j: (i, 0))]
        args += [g2, ws, bs]
    res = pl.pallas_call(
        functools.partial(_ffn_kernel, n_ff=n_ff, emit_mix=mix is not None),
        grid=(n // tm, n_ff),
        in_specs=in_specs,
        out_specs=out_specs,
        out_shape=out_shape,
        scratch_shapes=[pltpu.VMEM((tm, d), BF16), pltpu.VMEM((tm, d), F32)],
        compiler_params=_params("parallel", "arbitrary"),
        name="ffn_mix" if mix is not None else "ffn",
    )(*args)
    return res if mix is not None else res[0]


def _proj_kernel(a_ref, w_ref, b_ref, o_ref):
    o_ref[...] = (_dot(a_ref[...], w_ref[...]) + b_ref[...]).astype(o_ref.dtype)


def _proj(a, w, b, *, tm=1024, tn=1536):
    n, d = a.shape
    p = w.shape[1]
    assert n % tm == 0 and p % tn == 0
    return pl.pallas_call(
        _proj_kernel,
        grid=(p // tn, n // tm),
        in_specs=[
            pl.BlockSpec((tm, d), lambda j, i: (i, 0)),
            pl.BlockSpec((d, tn), lambda j, i: (0, j)),
            pl.BlockSpec((1, tn), lambda j, i: (0, j)),
        ],
        out_specs=pl.BlockSpec((tm, tn), lambda j, i: (i, j)),
        out_shape=jax.ShapeDtypeStruct((n, p), BF16),
        compiler_params=_params("parallel", "parallel"),
        name="proj",
    )(a, w, b)


def _prep_kernel(uk_ref, uv_ref, ks_ref, kw_ref, pk_ref, w1k_ref, w2k_ref, pv_ref, w1v_ref, w2v_ref,
                 gc_ref, gs_ref, gw_ref, kc_ref, vc_ref, ksa_ref, kwn_ref):
    half = w1k_ref.shape[0] // 2
    rows = uk_ref.shape[1]

    def compress(u_ref, pos_ref, w1_ref, w2_ref):
        u = u_ref[0]
        top = _dot(u, w1_ref[:half, :])
        bot = _dot(u, w1_ref[half:, :])
        pos = jnp.broadcast_to(pos_ref[...], (8, pos_ref.shape[1])).astype(BF16)
        pb = _dot(pos, w1_ref[...])[0:1, :]
        h = top + pltpu.roll(bot, rows - 1, axis=0) + pb
        return _dot(jax.nn.gelu(h).astype(BF16), w2_ref[...])

    kc = compress(uk_ref, pk_ref, w1k_ref, w2k_ref)
    kc_ref[0] = _rms(kc, gc_ref[...]).astype(BF16)
    vc_ref[0] = compress(uv_ref, pv_ref, w1v_ref, w2v_ref).astype(BF16)

    s = ks_ref.shape[1]
    dh = ks_ref.shape[2]
    ksn = _rms(ks_ref[0].astype(F32), gs_ref[...]).astype(BF16)
    key_blk = lax.broadcasted_iota(jnp.int32, (s, dh), 0) // SLC_BLOCK
    blk = lax.broadcasted_iota(jnp.int32, (s, dh), 1)
    onehot = (key_blk == blk).astype(BF16)
    ksa_ref[0] = jnp.concatenate([ksn, onehot], axis=1)
    kwn_ref[0] = _rms(kw_ref[0].astype(F32), gw_ref[...]).astype(BF16)


def _prep(uk, uv, ks, kw, pos_k, w1k, w2k, pos_v, w1v, w2v, gc, gs, gw):
    bg, rows, width = uk.shape
    s, dh = ks.shape[1], ks.shape[2]
    hid = w1k.shape[1]
    full = lambda shape: pl.BlockSpec(shape, lambda i: (0,) * len(shape))
    per = lambda shape: pl.BlockSpec((1,) + shape, lambda i: (i, 0, 0))
    return pl.pallas_call(
        _prep_kernel,
        grid=(bg,),
        in_specs=[per((rows, width)), per((rows, width)), per((s, dh)), per((s, dh)),
                  full((1, 2 * width)), full((2 * width, hid)), full((hid, dh)),
                  full((1, 2 * width)), full((2 * width, hid)), full((hid, dh)),
                  full((1, dh)), full((1, dh)), full((1, dh))],
        out_specs=[per((rows, dh)), per((rows, dh)), per((s, 2 * dh)), per((s, dh))],
        out_shape=[jax.ShapeDtypeStruct((bg, rows, dh), BF16), jax.ShapeDtypeStruct((bg, rows, dh), BF16),
                   jax.ShapeDtypeStruct((bg, s, 2 * dh), BF16), jax.ShapeDtypeStruct((bg, s, dh), BF16)],
        compiler_params=_params("parallel"),
        name="nsa_prep",
    )(uk, uv, ks, kw, pos_k, w1k, w2k, pos_v, w1v, w2v, gc, gs, gw)


def _attn_kernel(q_ref, gate_ref, kc_ref, vc_ref, ksa_ref, vs_ref, kw_ref, vw_ref, qg_ref, ovt_ref,
                 o_ref, m_ref, l_ref, acc_ref, *, tq):
    r = NSA_GROUP
    dh = NSA_HEAD_DIM
    m_rows = r * tq
    n_cmp = kc_ref.shape[1]
    i = pl.program_id(1)
    q0 = i * tq

    q = q_ref[0, 0].astype(F32).reshape(m_rows, dh)
    qn = (_rms(q, qg_ref[...]) * (dh ** -0.5)).astype(BF16)

    t_col = q0 + lax.broadcasted_iota(jnp.int32, (m_rows, 1), 0) % tq

    def local_mask(strictly_later):
        tl = lax.broadcasted_iota(jnp.int32, (m_rows, tq), 0) % tq
        kl = lax.broadcasted_iota(jnp.int32, (m_rows, tq), 1)
        return (kl > tl) if strictly_later else (kl <= tl)

    sc = lax.dot_general(qn, kc_ref[0], _NT, preferred_element_type=F32)
    c_end = lax.broadcasted_iota(jnp.int32, (1, n_cmp), 1) * CMP_STRIDE + (CMP_BLOCK - 1)
    cmask = c_end <= t_col
    sc = jnp.where(cmask, sc, NEG_INF)
    e = jnp.exp(sc - jnp.max(sc, axis=-1, keepdims=True))
    p_cmp = e / jnp.sum(e, axis=-1, keepdims=True) * cmask.astype(F32)
    o_cmp = _dot(p_cmp.astype(BF16), vc_ref[0])

    p_sum = p_cmp[0:tq]
    for h in range(1, r):
        p_sum = p_sum + p_cmp[h * tq:(h + 1) * tq]
    ovt = ovt_ref[...]
    imp_t = sum(lax.dot_general(ovt, part, _NT, preferred_element_type=F32) for part in _split3(p_sum))
    n_blk = imp_t.shape[0]
    blk = lax.broadcasted_iota(jnp.int32, (n_blk, 1), 0)
    t_row = q0 + lax.broadcasted_iota(jnp.int32, (1, tq), 1)
    cur = t_row // SLC_BLOCK
    forced = (blk == 0) | (blk == cur) | (blk == cur - 1)
    score = jnp.where(forced, FORCE_SCORE, jnp.where(blk * SLC_BLOCK <= t_row, imp_t, -1.0))
    rank = jnp.zeros((n_blk, tq), jnp.int32)
    for k in range(n_blk):
        sk = score[k:k + 1, :]
        ahead = (sk > score) | ((sk == score) & (blk > k))
        rank = rank + ahead.astype(jnp.int32)
    sel_bias_t = jnp.where(rank < SLC_TOPN, 0.0, NEG_INF)
    bias = jnp.concatenate([jnp.zeros((dh, tq), F32), sel_bias_t], axis=0).T
    bias4 = jnp.concatenate([bias] * r, axis=0)
    lane = lax.broadcasted_iota(jnp.int32, (m_rows, 2 * dh), 1)
    qpad = jnp.concatenate([qn, jnp.zeros_like(qn)], axis=1)
    qa = jnp.where(lane < dh, qpad, bias4.astype(BF16))

    def first_tile(qx, k_tile, v_tile):
        s = lax.dot_general(qx, k_tile, _NT, preferred_element_type=F32)
        s = jnp.where(local_mask(False), s, NEG_INF)
        mx = jnp.max(s, axis=-1, keepdims=True)
        p = jnp.exp(s - mx)
        m_ref[...] = mx
        l_ref[...] = jnp.sum(p, axis=-1, keepdims=True)
        acc_ref[...] = _dot(p.astype(BF16), v_tile)

    def next_tile(qx, k_tile, v_tile, mask):
        s = lax.dot_general(qx, k_tile, _NT, preferred_element_type=F32)
        if mask is not None:
            s = jnp.where(mask, s, NEG_INF)
        m_prev = m_ref[...]
        m_new = jnp.maximum(m_prev, jnp.max(s, axis=-1, keepdims=True))
        alpha = jnp.exp(m_prev - m_new)
        p = jnp.exp(s - m_new)
        m_ref[...] = m_new
        l_ref[...] = alpha * l_ref[...] + jnp.sum(p, axis=-1, keepdims=True)
        acc_ref[...] = alpha * acc_ref[...] + _dot(p.astype(BF16), v_tile)

    first_tile(qa, ksa_ref[0, pl.ds(q0, tq), :], vs_ref[0, pl.ds(q0, tq), :])

    def slc_body(kt, carry):
        k0 = pl.multiple_of(kt * tq, tq)
        next_tile(qa, ksa_ref[0, pl.ds(k0, tq), :], vs_ref[0, pl.ds(k0, tq), :], None)
        return carry

    lax.fori_loop(0, i, slc_body, 0)
    o_slc = acc_ref[...] / l_ref[...]

    first_tile(qn, kw_ref[0, pl.ds(q0, tq), :], vw_ref[0, pl.ds(q0, tq), :])
    n_back = WINDOW // tq
    for d in range(1, n_back + 1):
        @pl.when(i >= d)
        def _(d=d):
            k0 = pl.multiple_of(q0 - d * tq, tq)
            mask = local_mask(True) if d == n_back else None
            next_tile(qn, kw_ref[0, pl.ds(k0, tq), :], vw_ref[0, pl.ds(k0, tq), :], mask)
    o_win = acc_ref[...] / l_ref[...]

    gt = jax.nn.sigmoid(gate_ref[0, 0].reshape(m_rows, 3))
    out = gt[:, 0:1] * o_cmp + gt[:, 1:2] * o_slc + gt[:, 2:3] * o_win
    o_ref[0, 0] = out.reshape(r, tq, dh).astype(o_ref.dtype)


def _attn(q, gates, kc, vc, ksa, vs, kwn, vw, q_gain, ovt, *, tq=256):
    b, g, r, s, dh = q.shape
    assert s % tq == 0 and WINDOW % tq == 0 and s // SLC_BLOCK <= dh and tq % SLC_BLOCK == 0
    n_cmp = kc.shape[1]
    per_bg = lambda shape: pl.BlockSpec((1,) + shape, lambda bg, i: (bg, 0, 0))
    return pl.pallas_call(
        functools.partial(_attn_kernel, tq=tq),
        grid=(b * g, s // tq),
        in_specs=[
            pl.BlockSpec((1, 1, r, tq, dh), lambda bg, i: (bg // g, bg % g, 0, i, 0)),
            pl.BlockSpec((1, 1, r, tq, 3), lambda bg, i: (bg // g, bg % g, 0, i, 0)),
            per_bg((n_cmp, dh)), per_bg((n_cmp, dh)),
            per_bg((s, 2 * dh)), per_bg((s, dh)), per_bg((s, dh)), per_bg((s, dh)),
            pl.BlockSpec((1, dh), lambda bg, i: (0, 0)),
            pl.BlockSpec(ovt.shape, lambda bg, i: (0, 0)),
        ],
        out_specs=pl.BlockSpec((1, 1, r, tq, dh), lambda bg, i: (bg // g, bg % g, 0, i, 0)),
        out_shape=jax.ShapeDtypeStruct((b, g, r, s, dh), BF16),
        scratch_shapes=[pltpu.VMEM((r * tq, 1), F32), pltpu.VMEM((r * tq, 1), F32),
                        pltpu.VMEM((r * tq, dh), F32)],
        compiler_params=_params("parallel", "parallel"),
        name="nsa_attn",
    )(q, gates, kc, vc, ksa, vs, kwn, vw, q_gain, ovt)


def _mlstm_kernel(mq_ref, mk_ref, mv_ref, mo_ref, small_ref, gt_ref, cw_ref, cb_ref, og_ref, cum_ref,
                  o_ref, xbuf_ref, c_ref, n_ref, m_ref, *, tt):
    hd = M_HEAD_DIM
    L = M_CHUNK
    width = M_WIDTH
    n_chunk = tt // L
    j = pl.program_id(1)

    @pl.when(j == 0)
    def _():
        xbuf_ref[0:8, :] = jnp.zeros((8, 2 * width), F32)
        c_ref[...] = jnp.zeros_like(c_ref)
        n_ref[...] = jnp.zeros_like(n_ref)
        m_ref[...] = jnp.zeros_like(m_ref)

    @pl.when(j > 0)
    def _():
        xbuf_ref[0:8, :] = xbuf_ref[tt:tt + 8, :]

    xbuf_ref[8:8 + tt, 0:width] = mq_ref[...].astype(F32)
    xbuf_ref[8:8 + tt, width:2 * width] = mk_ref[...].astype(F32)
    conv = cb_ref[...]
    for tap in range(CONV_WIDTH):
        off = 8 - (CONV_WIDTH - 1) + tap
        conv = conv + cw_ref[tap:tap + 1, :] * xbuf_ref[off:off + tt, :]
    qk = conv * jax.nn.sigmoid(conv)
    q_all = qk[:, 0:width]
    k_all = qk[:, width:2 * width] * (hd ** -0.5)

    small = small_ref[...]
    logf_col = jax.nn.log_sigmoid(small)
    gt = gt_ref[0].reshape(n_chunk * 8, L)
    logf_row = jax.nn.log_sigmoid(gt)
    cum_u = cum_ref[0]
    cum_l = cum_ref[1]
    a_row_all = sum(_dot(part, cum_u) for part in _split3(logf_row))
    causal = lax.broadcasted_iota(jnp.int32, (L, L), 1) <= lax.broadcasted_iota(jnp.int32, (L, L), 0)

    for c in range(n_chunk):
        rows = slice(c * L, (c + 1) * L)
        a_col_c = sum(_dot(cum_l, part) for part in _split3(logf_col[rows]))
        for h in range(M_HEADS):
            lanes = slice(h * hd, (h + 1) * hd)
            q_c = q_all[rows, lanes]
            k_c = k_all[rows, lanes]
            v_c = mv_ref[rows, lanes]
            q_b = q_c.astype(BF16)
            a_j = a_col_c[:, _S_MF + h:_S_MF + h + 1]
            li_j = small[rows, _S_MI + h:_S_MI + h + 1]
            a_s = a_row_all[c * 8 + M_HEADS + h:c * 8 + M_HEADS + h + 1, :]
            li_s = gt[c * 8 + h:c * 8 + h + 1, :]
            g_c = a_j[L - 1:L, :]
            m_st = m_ref[h:h + 1, 0:1]
            c_st = c_ref[h]
            n_st = n_ref[h]

            log_w = jnp.where(causal, a_j - a_s + li_s, NEG_INF)
            m_intra = jnp.max(log_w, axis=-1, keepdims=True)
            w = jnp.exp(log_w - m_intra) * lax.dot_general(q_b, k_c.astype(BF16), _NT, preferred_element_type=F32)
            num_intra = _dot(w.astype(BF16), v_c)
            den_intra = jnp.sum(w, axis=-1, keepdims=True)

            num_inter = _dot(q_b, c_st.astype(BF16))
            den_inter = jnp.sum(q_c * n_st, axis=-1, keepdims=True)

            log_inter = a_j + m_st
            m_comb = jnp.maximum(log_inter, m_intra)
            s_inter = jnp.exp(log_inter - m_comb)
            s_intra = jnp.exp(m_intra - m_comb)
            num = s_inter * num_inter + s_intra * num_intra
            den = s_inter * den_inter + s_intra * den_intra
            hcell = num / jnp.maximum(jnp.abs(den), jnp.exp(-m_comb))

            log_u = g_c - a_j + li_j
            m_new = jnp.maximum(g_c + m_st, jnp.max(log_u, axis=0, keepdims=True))
            decay = jnp.exp(g_c + m_st - m_new)
            uk = jnp.exp(log_u - m_new) * k_c
            c_ref[h] = decay * c_st + lax.dot_general(uk.astype(BF16), v_c, _TN, preferred_element_type=F32)
            n_ref[h] = decay * n_st + jnp.sum(uk, axis=0, keepdims=True)
            m_ref[h:h + 1, :] = jnp.broadcast_to(m_new, (1, m_ref.shape[1]))

            hn = _rms(hcell, og_ref[:, lanes])
            o_gate = jax.nn.sigmoid(mo_ref[rows, lanes].astype(F32))
            o_ref[rows, lanes] = (o_gate * hn).astype(o_ref.dtype)


def _mlstm(proj, small, gate_t, conv_w, conv_b, out_gain, cum_u, *, batch, tt=256):
    n = proj.shape[0]
    s = n // batch
    assert s % tt == 0 and tt % M_CHUNK == 0
    nt = s // tt
    cq = _P_MQ // M_WIDTH
    assert cq * M_WIDTH == _P_MQ
    col = lambda k: pl.BlockSpec((tt, M_WIDTH), lambda b, j: (b * nt + j, k))
    full = lambda shape: pl.BlockSpec(shape, lambda b, j: (0,) * len(shape))
    return pl.pallas_call(
        functools.partial(_mlstm_kernel, tt=tt),
        grid=(batch, nt),
        in_specs=[col(cq), col(cq + 1), col(cq + 2), col(cq + 3),
                  pl.BlockSpec((tt, _SMALL_WIDTH), lambda b, j: (b * nt + j, 0)),
                  pl.BlockSpec((1, tt // M_CHUNK, 8, M_CHUNK), lambda b, j: (b, j, 0, 0)),
                  full(conv_w.shape), full(conv_b.shape), full(out_gain.shape), full(cum_u.shape)],
        out_specs=pl.BlockSpec((tt, M_WIDTH), lambda b, j: (b * nt + j, 0)),
        out_shape=jax.ShapeDtypeStruct((n, M_WIDTH), BF16),
        scratch_shapes=[pltpu.VMEM((tt + 8, 2 * M_WIDTH), F32),
                        pltpu.VMEM((M_HEADS, M_HEAD_DIM, M_HEAD_DIM), F32),
                        pltpu.VMEM((M_HEADS, 1, M_HEAD_DIM), F32),
                        pltpu.VMEM((8, 128), F32)],
        compiler_params=_params("parallel", "arbitrary"),
        name="mlstm",
    )(proj, proj, proj, proj, small, gate_t, conv_w, conv_b, out_gain, cum_u)


def _merge_kernel(x_ref, oa_ref, ob_ref, ga_ref, gb_ref, wa_ref, wb_ref, wo_ref, o_ref):
    ya = _dot(oa_ref[...], wa_ref[...])
    yb = _dot(ob_ref[...], wb_ref[...])
    merged = (jax.nn.sigmoid(ga_ref[...].astype(F32)) * ya
              + jax.nn.sigmoid(gb_ref[...].astype(F32)) * yb)
    o_ref[...] = x_ref[...] + _dot(merged.astype(BF16), wo_ref[...])


def _merge(x, o_nsa, h_m, proj, wa, wb, wo, *, tm=256):
    n, d = x.shape
    assert n % tm == 0 and _P_MERGE == 0
    const = lambda shape: pl.BlockSpec(shape, lambda i: (0, 0), pipeline_mode=pl.Buffered(1))
    return pl.pallas_call(
        _merge_kernel,
        grid=(n // tm,),
        in_specs=[
            pl.BlockSpec((tm, d), lambda i: (i, 0)),
            pl.BlockSpec((tm, o_nsa.shape[1]), lambda i: (i, 0)),
            pl.BlockSpec((tm, h_m.shape[1]), lambda i: (i, 0)),
            pl.BlockSpec((tm, d), lambda i: (i, 0)),
            pl.BlockSpec((tm, d), lambda i: (i, 1)),
            const(wa.shape), const(wb.shape), const(wo.shape),
        ],
        out_specs=pl.BlockSpec((tm, d), lambda i: (i, 0)),
        out_shape=jax.ShapeDtypeStruct((n, d), F32),
        compiler_params=_params("parallel"),
        name="merge",
    )(x, o_nsa, h_m, proj, proj, wa, wb, wo)


def _layer(x2, b, s, p):
    n = x2.shape[0]
    g, r, dh = NSA_KV_HEADS, NSA_GROUP, NSA_HEAD_DIM
    w_in, b_in = p["w_in"], p["b_in"]

    def cols(a, start, size):
        return a[..., start:start + size]

    order = [(_OFF_MERGE, 2 * D_MODEL), (_OFF_Q, NSA_Q_WIDTH), (_OFF_MQKV, 3 * M_WIDTH),
             (_OFF_MO, M_WIDTH), (_OFF_KV, 6 * NSA_KV_WIDTH)]
    w_main = jnp.concatenate([cols(w_in, o, z) for o, z in order], axis=1).astype(BF16)
    b_main = jnp.concatenate([cols(b_in, o, z) for o, z in order], axis=0)[None, :]
    small_order = [(_OFF_G, 3 * NSA_HEADS), (_OFF_MI, M_HEADS), (_OFF_MF, M_HEADS)]
    n_small = sum(z for _, z in small_order)
    w_small = jnp.pad(jnp.concatenate([cols(w_in, o, z) for o, z in small_order], axis=1),
                      ((0, 0), (0, _SMALL_WIDTH - n_small))).astype(BF16)
    b_small = jnp.pad(jnp.concatenate([cols(b_in, o, z) for o, z in small_order], axis=0),
                      (0, _SMALL_WIDTH - n_small))[None, :]

    x1, hn, small = _ffn(x2, p["ffn1_norm"][None, :], p["ffn1_w_gate"].astype(BF16),
                         p["ffn1_w_up"].astype(BF16), p["ffn1_w_down"].astype(BF16),
                         mix=(p["mix_norm"][None, :], w_small, b_small))
    proj = _proj(hn, w_main, b_main)

    q = proj[:, _P_Q:_P_Q + NSA_Q_WIDTH].reshape(b, s, g, r, dh).transpose(0, 2, 3, 1, 4)
    kv = proj[:, _P_KV:_P_KV + 6 * NSA_KV_WIDTH].reshape(b, s, 6, g, dh).transpose(2, 0, 3, 1, 4)
    kv = kv.reshape(6, b * g, s, dh)
    gates = small[:, 0:3 * NSA_HEADS].reshape(b, s, g, r, 3).transpose(0, 2, 3, 1, 4)
    rows = s // CMP_STRIDE
    blk_w = CMP_STRIDE * dh
    kc, vc, ksa, kwn = _prep(
        kv[0].reshape(b * g, rows, blk_w), kv[1].reshape(b * g, rows, blk_w), kv[2], kv[4],
        p["cmp_pos_k"].reshape(1, CMP_BLOCK * dh), p["cmp_w1_k"].astype(BF16), p["cmp_w2_k"].astype(BF16),
        p["cmp_pos_v"].reshape(1, CMP_BLOCK * dh), p["cmp_w1_v"].astype(BF16), p["cmp_w2_v"].astype(BF16),
        p["nsa_kc_gain"][None, :], p["nsa_ks_gain"][None, :], p["nsa_kw_gain"][None, :])
    c_start = np.arange(rows) * CMP_STRIDE
    b_start = np.arange(dh) * SLC_BLOCK
    ovt = ((c_start[None, :] < (b_start + SLC_BLOCK)[:, None]) & ((c_start + CMP_BLOCK)[None, :] > b_start[:, None])
           & (np.arange(rows) < rows - 1)[None, :])
    o_nsa = _attn(q, gates, kc, vc, ksa, kv[3], kwn, kv[5], p["nsa_q_gain"][None, :],
                  jnp.asarray(ovt, BF16))
    o_nsa = o_nsa.transpose(0, 3, 1, 2, 4).reshape(n, NSA_Q_WIDTH)

    gate_t = small[:, _S_MI:_S_MI + 2 * M_HEADS].reshape(b, s // M_CHUNK, M_CHUNK, 2 * M_HEADS)
    gate_t = gate_t.transpose(0, 1, 3, 2)
    upper = np.triu(np.ones((M_CHUNK, M_CHUNK), np.float32))
    cum_u = jnp.asarray(np.stack([upper, upper.T]), BF16)
    h_m = _mlstm(proj, small, gate_t, p["m_conv_w"], p["m_conv_b"][None, :],
                 p["m_out_gain"].reshape(1, M_WIDTH), cum_u, batch=b)

    x3 = _merge(x1, o_nsa, h_m, proj, p["w_branch_nsa"].astype(BF16), p["w_branch_mlstm"].astype(BF16),
                p["w_out"].astype(BF16))
    return _ffn(x3, p["ffn2_norm"][None, :], p["ffn2_w_gate"].astype(BF16),
                p["ffn2_w_up"].astype(BF16), p["ffn2_w_down"].astype(BF16))


def kernel(x, ffn1_norm, ffn1_w_gate, ffn1_w_up, ffn1_w_down, mix_norm, w_in, b_in, nsa_q_gain, nsa_kc_gain, nsa_ks_gain, nsa_kw_gain, cmp_pos_k, cmp_w1_k, cmp_w2_k, cmp_pos_v, cmp_w1_v, cmp_w2_v, m_conv_w, m_conv_b, m_out_gain, w_branch_nsa, w_branch_mlstm, w_out, ffn2_norm, ffn2_w_gate, ffn2_w_up, ffn2_w_down):
    params = dict(ffn1_norm=ffn1_norm, ffn1_w_gate=ffn1_w_gate, ffn1_w_up=ffn1_w_up, ffn1_w_down=ffn1_w_down,
                  mix_norm=mix_norm, w_in=w_in, b_in=b_in, nsa_q_gain=nsa_q_gain, nsa_kc_gain=nsa_kc_gain,
                  nsa_ks_gain=nsa_ks_gain, nsa_kw_gain=nsa_kw_gain, cmp_pos_k=cmp_pos_k, cmp_w1_k=cmp_w1_k,
                  cmp_w2_k=cmp_w2_k, cmp_pos_v=cmp_pos_v, cmp_w1_v=cmp_w1_v, cmp_w2_v=cmp_w2_v,
                  m_conv_w=m_conv_w, m_conv_b=m_conv_b, m_out_gain=m_out_gain, w_branch_nsa=w_branch_nsa,
                  w_branch_mlstm=w_branch_mlstm, w_out=w_out, ffn2_norm=ffn2_norm, ffn2_w_gate=ffn2_w_gate,
                  ffn2_w_up=ffn2_w_up, ffn2_w_down=ffn2_w_down)
    b, s, d = x.shape
    h = x.reshape(b * s, d)
    for l in range(ffn1_norm.shape[0]):
        h = _layer(h, b, s, {k: v[l] for k, v in params.items()})
    return h.reshape(b, s, d)
```

```python
import functools

import numpy as np
import jax
import jax.numpy as jnp
from jax import lax
from jax.experimental import pallas as pl
from jax.experimental.pallas import tpu as pltpu

F32 = jnp.float32
BF16 = jnp.bfloat16

D_MODEL = 2048
D_FF = 5632
NSA_HEADS = 16
NSA_KV_HEADS = 4
NSA_GROUP = NSA_HEADS // NSA_KV_HEADS
NSA_HEAD_DIM = 64
NSA_Q_WIDTH = NSA_HEADS * NSA_HEAD_DIM
NSA_KV_WIDTH = NSA_KV_HEADS * NSA_HEAD_DIM
CMP_BLOCK = 32
CMP_STRIDE = 16
CMP_HIDDEN = 256
SLC_BLOCK = 64
SLC_TOPN = 16
WINDOW = 512
M_HEADS = 4
M_HEAD_DIM = 256
M_WIDTH = M_HEADS * M_HEAD_DIM
M_CHUNK = 64
CONV_WIDTH = 4
EPS = 1e-6
NEG_INF = -1e30
FORCE_SCORE = 1e4
LOG2_E = 1.4426950408889634

_OFF_Q = 0
_OFF_KV = _OFF_Q + NSA_Q_WIDTH
_OFF_G = _OFF_KV + 6 * NSA_KV_WIDTH
_OFF_MQKV = _OFF_G + 3 * NSA_HEADS
_OFF_MI = _OFF_MQKV + 3 * M_WIDTH
_OFF_MF = _OFF_MI + M_HEADS
_OFF_MO = _OFF_MF + M_HEADS
_OFF_MERGE = _OFF_MO + M_WIDTH
_IN_WIDTH = _OFF_MERGE + 2 * D_MODEL

_P_MERGE = 0
_P_Q = 2 * D_MODEL
_P_MQ = _P_Q + NSA_Q_WIDTH
_P_MO = _P_MQ + 3 * M_WIDTH
_P_KV = _P_MO + M_WIDTH
_P_WIDTH = _P_KV + 6 * NSA_KV_WIDTH
_SMALL_WIDTH = 128
_S_MI = 3 * NSA_HEADS
_S_MF = _S_MI + M_HEADS

VMEM_LIMIT_BYTES = 56 * 1024 * 1024
_ATTN_TQ = 256
_ATTN_PAIR = 2

_NT = (((1,), (1,)), ((), ()))
_TN = (((0,), (0,)), ((), ()))


def _params(*sem):
    return pltpu.CompilerParams(dimension_semantics=sem, vmem_limit_bytes=VMEM_LIMIT_BYTES)


def _dot(a, b):
    return jnp.dot(a, b, preferred_element_type=F32)


def _split3(x):
    hi = x.astype(BF16)
    r1 = x - hi.astype(F32)
    mid = r1.astype(BF16)
    lo = (r1 - mid.astype(F32)).astype(BF16)
    return hi, mid, lo


def _rms(x, gain):
    return x * lax.rsqrt(jnp.mean(x * x, axis=-1, keepdims=True) + EPS) * gain


def _ffn_kernel(*refs, n_ff, emit_mix):
    if emit_mix:
        (x_ref, g_ref, wg_ref, wu_ref, wd_ref, g2_ref, ws_ref, bs_ref,
         o_ref, hn2_ref, small_ref, hn_ref, acc_ref) = refs
    else:
        x_ref, g_ref, wg_ref, wu_ref, wd_ref, o_ref, hn_ref, acc_ref = refs
    j = pl.program_id(1)

    @pl.when(j == 0)
    def _():
        hn_ref[...] = _rms(x_ref[...], g_ref[...]).astype(BF16)
        acc_ref[...] = jnp.zeros_like(acc_ref)

    hn = hn_ref[...]
    a = _dot(hn, wg_ref[...])
    b = _dot(hn, wu_ref[...])
    h = (a * jax.nn.sigmoid(a) * b).astype(BF16)
    acc_ref[...] += _dot(h, wd_ref[...])

    @pl.when(j == n_ff - 1)
    def _():
        y = x_ref[...] + 0.5 * acc_ref[...]
        o_ref[...] = y
        if emit_mix:
            hn2 = _rms(y, g2_ref[...]).astype(BF16)
            hn2_ref[...] = hn2
            small_ref[...] = _dot(hn2, ws_ref[...]) + bs_ref[...]


def _ffn(x, gain, wg, wu, wd, mix=None, *, tm=512, tf=512):
    n, d = x.shape
    f = wg.shape[1]
    assert n % tm == 0 and f % tf == 0
    n_ff = f // tf
    in_specs = [
        pl.BlockSpec((tm, d), lambda i, j: (i, 0)),
        pl.BlockSpec((1, d), lambda i, j: (0, 0)),
        pl.BlockSpec((d, tf), lambda i, j: (0, j)),
        pl.BlockSpec((d, tf), lambda i, j: (0, j)),
        pl.BlockSpec((tf, d), lambda i, j: (j, 0)),
    ]
    out_shape = [jax.ShapeDtypeStruct((n, d), F32)]
    out_specs = [pl.BlockSpec((tm, d), lambda i, j: (i, 0))]
    args = [x, gain, wg, wu, wd]
    if mix is not None:
        g2, ws, bs = mix
        in_specs += [
            pl.BlockSpec((1, d), lambda i, j: (0, 0)),
            pl.BlockSpec((d, _SMALL_WIDTH), lambda i, j: (0, 0)),
            pl.BlockSpec((1, _SMALL_WIDTH), lambda i, j: (0, 0)),
        ]
        out_shape += [jax.ShapeDtypeStruct((n, d), BF16), jax.ShapeDtypeStruct((n, _SMALL_WIDTH), F32)]
        out_specs += [pl.BlockSpec((tm, d), lambda i, j: (i, 0)),
                      pl.BlockSpec((tm, _SMALL_WIDTH), lambda i, j: (i, 0))]
        args += [g2, ws, bs]
    res = pl.pallas_call(
        functools.partial(_ffn_kernel, n_ff=n_ff, emit_mix=mix is not None),
        grid=(n // tm, n_ff),
        in_specs=in_specs,
        out_specs=out_specs,
        out_shape=out_shape,
        scratch_shapes=[pltpu.VMEM((tm, d), BF16), pltpu.VMEM((tm, d), F32)],
        compiler_params=_params("parallel", "arbitrary"),
        name="ffn_mix" if mix is not None else "ffn",
    )(*args)
    return res if mix is not None else res[0]


def _proj_kernel(a_ref, w_ref, b_ref, o_ref):
    o_ref[...] = (_dot(a_ref[...], w_ref[...]) + b_ref[...]).astype(o_ref.dtype)


def _proj(a, w, b, *, tm=1024, tn=1536):
    n, d = a.shape
    p = w.shape[1]
    assert n % tm == 0 and p % tn == 0
    return pl.pallas_call(
        _proj_kernel,
        grid=(p // tn, n // tm),
        in_specs=[
            pl.BlockSpec((tm, d), lambda j, i: (i, 0)),
            pl.BlockSpec((d, tn), lambda j, i: (0, j)),
            pl.BlockSpec((1, tn), lambda j, i: (0, j)),
        ],
        out_specs=pl.BlockSpec((tm, tn), lambda j, i: (i, j)),
        out_shape=jax.ShapeDtypeStruct((n, p), BF16),
        compiler_params=_params("parallel", "parallel"),
        name="proj",
    )(a, w, b)


def _prep_kernel(uk_ref, uv_ref, ks_ref, kw_ref, pk_ref, w1k_ref, w2k_ref, pv_ref, w1v_ref, w2v_ref,
                 gc_ref, gs_ref, gw_ref, kc_ref, vc_ref, ksa_ref, kwn_ref):
    half = w1k_ref.shape[0] // 2
    rows = uk_ref.shape[1]

    def compress(u_ref, pos_ref, w1_ref, w2_ref):
        u = u_ref[0]
        top = _dot(u, w1_ref[:half, :])
        bot = _dot(u, w1_ref[half:, :])
        pos = jnp.broadcast_to(pos_ref[...], (8, pos_ref.shape[1])).astype(BF16)
        pb = _dot(pos, w1_ref[...])[0:1, :]
        h = top + pltpu.roll(bot, rows - 1, axis=0) + pb
        return _dot(jax.nn.gelu(h).astype(BF16), w2_ref[...])

    kc = compress(uk_ref, pk_ref, w1k_ref, w2k_ref)
    kc_ref[0] = _rms(kc, gc_ref[...]).astype(BF16)
    vc_ref[0] = compress(uv_ref, pv_ref, w1v_ref, w2v_ref).astype(BF16)

    s = ks_ref.shape[1]
    dh = ks_ref.shape[2]
    ksn = _rms(ks_ref[0].astype(F32), gs_ref[...]).astype(BF16)
    key_blk = lax.broadcasted_iota(jnp.int32, (s, dh), 0) // SLC_BLOCK
    blk = lax.broadcasted_iota(jnp.int32, (s, dh), 1)
    onehot = (key_blk == blk).astype(BF16)
    ksa_ref[0] = jnp.concatenate([ksn, onehot], axis=1)
    kwn_ref[0] = _rms(kw_ref[0].astype(F32), gw_ref[...]).astype(BF16)


def _prep(uk, uv, ks, kw, pos_k, w1k, w2k, pos_v, w1v, w2v, gc, gs, gw):
    bg, rows, width = uk.shape
    s, dh = ks.shape[1], ks.shape[2]
    hid = w1k.shape[1]
    full = lambda shape: pl.BlockSpec(shape, lambda i: (0,) * len(shape))
    per = lambda shape: pl.BlockSpec((1,) + shape, lambda i: (i, 0, 0))
    return pl.pallas_call(
        _prep_kernel,
        grid=(bg,),
        in_specs=[per((rows, width)), per((rows, width)), per((s, dh)), per((s, dh)),
                  full((1, 2 * width)), full((2 * width, hid)), full((hid, dh)),
                  full((1, 2 * width)), full((2 * width, hid)), full((hid, dh)),
                  full((1, dh)), full((1, dh)), full((1, dh))],
        out_specs=[per((rows, dh)), per((rows, dh)), per((s, 2 * dh)), per((s, dh))],
        out_shape=[jax.ShapeDtypeStruct((bg, rows, dh), BF16), jax.ShapeDtypeStruct((bg, rows, dh), BF16),
                   jax.ShapeDtypeStruct((bg, s, 2 * dh), BF16), jax.ShapeDtypeStruct((bg, s, dh), BF16)],
        compiler_params=_params("parallel"),
        name="nsa_prep",
    )(uk, uv, ks, kw, pos_k, w1k, w2k, pos_v, w1v, w2v, gc, gs, gw)


def _attn_kernel(q_ref, gate_ref, kc_ref, vct_ref, ksa_ref, vst_ref, kw_ref, vwt_ref, qg_ref, ovt_ref,
                 o_ref, qa_ref, m_ref, l_ref, acc_ref, ocmp_ref, score_ref, rank_ref, *, tq, pair):
    r = NSA_GROUP
    dh = NSA_HEAD_DIM
    n_cmp = kc_ref.shape[1]
    i = pl.program_id(1)
    q0 = i * tq
    head_lanes = [slice(h * tq, (h + 1) * tq) for h in range(r)]

    q = q_ref[0, 0, 0].astype(F32)
    inv = lax.rsqrt(jnp.mean(q * q, axis=0, keepdims=True) + EPS)
    qa_ref[0:dh, :] = (q * inv * qg_ref[...] * (dh ** -0.5 * LOG2_E)).astype(BF16)

    tq_row = q0 + lax.broadcasted_iota(jnp.int32, (1, tq), 1)

    c_end = lax.broadcasted_iota(jnp.int32, (n_cmp, 1), 0) * CMP_STRIDE + (CMP_BLOCK - 1)
    cmask = c_end <= tq_row
    has_cmp = tq_row >= CMP_BLOCK - 1
    p_sum = None
    for lanes in head_lanes:
        sc = jnp.where(cmask, _dot(kc_ref[0], qa_ref[0:dh, lanes]), NEG_INF)
        e = jnp.exp2(sc - jnp.max(sc, axis=0, keepdims=True))
        p_cmp = e * jnp.where(has_cmp, 1.0 / jnp.sum(e, axis=0, keepdims=True), 0.0)
        ocmp_ref[:, lanes] = _dot(vct_ref[0], p_cmp.astype(BF16))
        p_sum = p_cmp if p_sum is None else p_sum + p_cmp
    ovt = ovt_ref[...]
    imp_t = sum(_dot(ovt, part) for part in _split3(p_sum))
    n_blk = imp_t.shape[0]
    blk = lax.broadcasted_iota(jnp.int32, (n_blk, 1), 0)
    cur = tq_row // SLC_BLOCK
    forced = (blk == 0) | (blk == cur) | (blk == cur - 1)
    score_ref[...] = jnp.where(forced, FORCE_SCORE, jnp.where(blk * SLC_BLOCK <= tq_row, imp_t, -1.0))
    rank_ref[...] = jnp.zeros_like(rank_ref)
    sub = 8
    last_blk = (q0 + tq - 1) // SLC_BLOCK
    for kg in range(0, n_blk, sub):
        @pl.when(kg <= last_blk)
        def _(kg=kg):
            for ng in range(0, n_blk, sub):
                sn = score_ref[ng:ng + sub, :]
                rn = rank_ref[ng:ng + sub, :]
                for k in range(kg, kg + sub):
                    sk = score_ref[k:k + 1, :]
                    if ng > kg:
                        ahead = sk >= sn
                    elif ng < kg:
                        ahead = sk > sn
                    else:
                        later = lax.broadcasted_iota(jnp.int32, (sub, 1), 0) > (k - kg)
                        ahead = (sk > sn) | (later & (sk == sn))
                    rn = jnp.where(ahead, rn + 1.0, rn)
                rank_ref[ng:ng + sub, :] = rn
    sel_bias = jnp.where(rank_ref[...] < SLC_TOPN, 0.0, NEG_INF).astype(BF16)
    for lanes in head_lanes:
        qa_ref[dh:2 * dh, lanes] = sel_bias

    def key_block(k_ref, vt_ref, kdim, k0, n_keys, window, init):
        start = pl.multiple_of(k0, tq)
        k_blk = k_ref[0, pl.ds(start, n_keys), :]
        vt = vt_ref[0, :, pl.ds(start, n_keys)]
        ss = [_dot(k_blk, qa_ref[0:kdim, lanes]) for lanes in head_lanes]
        if window is not None:
            back = tq_row - (start + lax.broadcasted_iota(jnp.int32, (n_keys, 1), 0))
            mask = back >= 0
            if window != float("inf"):
                mask = mask & (back < window)
            ss = [jnp.where(mask, s, NEG_INF) for s in ss]
        mxs = [jnp.max(s, axis=0, keepdims=True) for s in ss]
        if init:
            ps = [jnp.exp2(s - mx) for s, mx in zip(ss, mxs)]
            for lanes, mx, p in zip(head_lanes, mxs, ps):
                m_ref[:, lanes] = mx
                l_ref[:, lanes] = jnp.sum(p, axis=0, keepdims=True)
                acc_ref[:, lanes] = _dot(vt, p.astype(BF16))
        else:
            m_prevs = [m_ref[:, lanes] for lanes in head_lanes]
            m_news = [jnp.maximum(mp, mx) for mp, mx in zip(m_prevs, mxs)]
            alphas = [jnp.exp2(mp - mn) for mp, mn in zip(m_prevs, m_news)]
            ps = [jnp.exp2(s - mn) for s, mn in zip(ss, m_news)]
            pvs = [_dot(vt, p.astype(BF16)) for p in ps]
            for lanes, mn, al, p, pv in zip(head_lanes, m_news, alphas, ps, pvs):
                m_ref[:, lanes] = mn
                l_ref[:, lanes] = al * l_ref[:, lanes] + jnp.sum(p, axis=0, keepdims=True)
                acc_ref[:, lanes] = al * acc_ref[:, lanes] + pv

    key_block(ksa_ref, vst_ref, 2 * dh, q0, tq, float("inf"), True)

    def slc_body(kt, carry):
        key_block(ksa_ref, vst_ref, 2 * dh, kt * (pair * tq), pair * tq, None, False)
        return carry

    lax.fori_loop(0, i // pair, slc_body, 0)
    for rem in range(1, pair):
        @pl.when(i % pair >= rem)
        def _(rem=rem):
            key_block(ksa_ref, vst_ref, 2 * dh, (i - rem) * tq, tq, None, False)
    gt = jax.nn.sigmoid(gate_ref[0, 0, 0])
    ocmp_ref[...] = gt[0:1, :] * ocmp_ref[...] + gt[1:2, :] * (acc_ref[...] / l_ref[...])

    key_block(kw_ref, vwt_ref, dh, jnp.maximum(q0 - WINDOW, 0), WINDOW + tq, WINDOW, True)
    o_ref[0, 0, 0] = (ocmp_ref[...] + gt[2:3, :] * (acc_ref[...] / l_ref[...])).astype(o_ref.dtype)


def _attn(qt, gates_t, kc, vct, ksa, vst, kwn, vwt, q_gain_col, ovt, *, tq):
    b, g, nq, dh, m_cols = qt.shape
    s = nq * tq
    assert WINDOW % tq == 0 and s >= WINDOW + tq and s // SLC_BLOCK <= dh and tq % SLC_BLOCK == 0
    n_cmp = kc.shape[1]
    per_bg = lambda shape: pl.BlockSpec((1,) + shape, lambda bg, i: (bg, 0, 0))
    tile = lambda rows: pl.BlockSpec((1, 1, 1, rows, m_cols), lambda bg, i: (bg // g, bg % g, i, 0, 0))
    return pl.pallas_call(
        functools.partial(_attn_kernel, tq=tq, pair=_ATTN_PAIR),
        grid=(b * g, nq),
        in_specs=[
            tile(dh), tile(3),
            per_bg((n_cmp, dh)), per_bg((dh, n_cmp)),
            per_bg((s, 2 * dh)), per_bg((dh, s)), per_bg((s, dh)), per_bg((dh, s)),
            pl.BlockSpec((dh, 1), lambda bg, i: (0, 0)),
            pl.BlockSpec(ovt.shape, lambda bg, i: (0, 0)),
        ],
        out_specs=tile(dh),
        out_shape=jax.ShapeDtypeStruct((b, g, nq, dh, m_cols), BF16),
        scratch_shapes=[pltpu.VMEM((2 * dh, m_cols), BF16),
                        pltpu.VMEM((1, m_cols), F32), pltpu.VMEM((1, m_cols), F32),
                        pltpu.VMEM((dh, m_cols), F32), pltpu.VMEM((dh, m_cols), F32),
                        pltpu.VMEM((dh, tq), F32), pltpu.VMEM((dh, tq), F32)],
        compiler_params=_params("parallel", "parallel"),
        name="nsa_attn",
    )(qt, gates_t, kc, vct, ksa, vst, kwn, vwt, q_gain_col, ovt)


def _mlstm_kernel(mq_ref, mk_ref, mv_ref, mo_ref, small_ref, gt_ref, cw_ref, cb_ref, og_ref, cum_ref,
                  o_ref, xbuf_ref, c_ref, n_ref, m_ref, *, tt):
    hd = M_HEAD_DIM
    L = M_CHUNK
    width = M_WIDTH
    n_chunk = tt // L
    j = pl.program_id(1)

    @pl.when(j == 0)
    def _():
        xbuf_ref[0:8, :] = jnp.zeros((8, 2 * width), F32)
        c_ref[...] = jnp.zeros_like(c_ref)
        n_ref[...] = jnp.zeros_like(n_ref)
        m_ref[...] = jnp.zeros_like(m_ref)

    @pl.when(j > 0)
    def _():
        xbuf_ref[0:8, :] = xbuf_ref[tt:tt + 8, :]

    xbuf_ref[8:8 + tt, 0:width] = mq_ref[...].astype(F32)
    xbuf_ref[8:8 + tt, width:2 * width] = mk_ref[...].astype(F32)
    conv = cb_ref[...]
    for tap in range(CONV_WIDTH):
        off = 8 - (CONV_WIDTH - 1) + tap
        conv = conv + cw_ref[tap:tap + 1, :] * xbuf_ref[off:off + tt, :]
    qk = conv * jax.nn.sigmoid(conv)
    q_all = qk[:, 0:width]
    k_all = qk[:, width:2 * width] * (hd ** -0.5)

    small = small_ref[...]
    logf_col = jax.nn.log_sigmoid(small)
    gt = gt_ref[0].reshape(n_chunk * 8, L)
    logf_row = jax.nn.log_sigmoid(gt)
    cum_u = cum_ref[0]
    cum_l = cum_ref[1]
    a_row_all = sum(_dot(part, cum_u) for part in _split3(logf_row))
    causal = lax.broadcasted_iota(jnp.int32, (L, L), 1) <= lax.broadcasted_iota(jnp.int32, (L, L), 0)

    for c in range(n_chunk):
        rows = slice(c * L, (c + 1) * L)
        a_col_c = sum(_dot(cum_l, part) for part in _split3(logf_col[rows]))
        for h in range(M_HEADS):
            lanes = slice(h * hd, (h + 1) * hd)
            q_c = q_all[rows, lanes]
            k_c = k_all[rows, lanes]
            v_c = mv_ref[rows, lanes]
            q_b = q_c.astype(BF16)
            a_j = a_col_c[:, _S_MF + h:_S_MF + h + 1]
            li_j = small[rows, _S_MI + h:_S_MI + h + 1]
            a_s = a_row_all[c * 8 + M_HEADS + h:c * 8 + M_HEADS + h + 1, :]
            li_s = gt[c * 8 + h:c * 8 + h + 1, :]
            g_c = a_j[L - 1:L, :]
            m_st = m_ref[h:h + 1, 0:1]
            c_st = c_ref[h]
            n_st = n_ref[h]

            log_w = jnp.where(causal, a_j - a_s + li_s, NEG_INF)
            m_intra = jnp.max(log_w, axis=-1, keepdims=True)
            w = jnp.exp(log_w - m_intra) * lax.dot_general(q_b, k_c.astype(BF16), _NT, preferred_element_type=F32)
            num_intra = _dot(w.astype(BF16), v_c)
            den_intra = jnp.sum(w, axis=-1, keepdims=True)

            num_inter = _dot(q_b, c_st.astype(BF16))
            den_inter = jnp.sum(q_c * n_st, axis=-1, keepdims=True)

            log_inter = a_j + m_st
            m_comb = jnp.maximum(log_inter, m_intra)
            s_inter = jnp.exp(log_inter - m_comb)
            s_intra = jnp.exp(m_intra - m_comb)
            num = s_inter * num_inter + s_intra * num_intra
            den = s_inter * den_inter + s_intra * den_intra
            hcell = num / jnp.maximum(jnp.abs(den), jnp.exp(-m_comb))

            log_u = g_c - a_j + li_j
            m_new = jnp.maximum(g_c + m_st, jnp.max(log_u, axis=0, keepdims=True))
            decay = jnp.exp(g_c + m_st - m_new)
            uk = jnp.exp(log_u - m_new) * k_c
            c_ref[h] = decay * c_st + lax.dot_general(uk.astype(BF16), v_c, _TN, preferred_element_type=F32)
            n_ref[h] = decay * n_st + jnp.sum(uk, axis=0, keepdims=True)
            m_ref[h:h + 1, :] = jnp.broadcast_to(m_new, (1, m_ref.shape[1]))

            hn = _rms(hcell, og_ref[:, lanes])
            o_gate = jax.nn.sigmoid(mo_ref[rows, lanes].astype(F32))
            o_ref[rows, lanes] = (o_gate * hn).astype(o_ref.dtype)


def _mlstm(proj, small, gate_t, conv_w, conv_b, out_gain, cum_u, *, batch, tt=256):
    n = proj.shape[0]
    s = n // batch
    assert s % tt == 0 and tt % M_CHUNK == 0
    nt = s // tt
    cq = _P_MQ // M_WIDTH
    assert cq * M_WIDTH == _P_MQ
    col = lambda k: pl.BlockSpec((tt, M_WIDTH), lambda b, j: (b * nt + j, k))
    full = lambda shape: pl.BlockSpec(shape, lambda b, j: (0,) * len(shape))
    return pl.pallas_call(
        functools.partial(_mlstm_kernel, tt=tt),
        grid=(batch, nt),
        in_specs=[col(cq), col(cq + 1), col(cq + 2), col(cq + 3),
                  pl.BlockSpec((tt, _SMALL_WIDTH), lambda b, j: (b * nt + j, 0)),
                  pl.BlockSpec((1, tt // M_CHUNK, 8, M_CHUNK), lambda b, j: (b, j, 0, 0)),
                  full(conv_w.shape), full(conv_b.shape), full(out_gain.shape), full(cum_u.shape)],
        out_specs=pl.BlockSpec((tt, M_WIDTH), lambda b, j: (b * nt + j, 0)),
        out_shape=jax.ShapeDtypeStruct((n, M_WIDTH), BF16),
        scratch_shapes=[pltpu.VMEM((tt + 8, 2 * M_WIDTH), F32),
                        pltpu.VMEM((M_HEADS, M_HEAD_DIM, M_HEAD_DIM), F32),
                        pltpu.VMEM((M_HEADS, 1, M_HEAD_DIM), F32),
                        pltpu.VMEM((8, 128), F32)],
        compiler_params=_params("parallel", "arbitrary"),
        name="mlstm",
    )(proj, proj, proj, proj, small, gate_t, conv_w, conv_b, out_gain, cum_u)


def _merge_kernel(x_ref, oa_ref, ob_ref, ga_ref, gb_ref, wa_ref, wb_ref, wo_ref, o_ref):
    ya = _dot(oa_ref[...], wa_ref[...])
    yb = _dot(ob_ref[...], wb_ref[...])
    merged = (jax.nn.sigmoid(ga_ref[...].astype(F32)) * ya
              + jax.nn.sigmoid(gb_ref[...].astype(F32)) * yb)
    o_ref[...] = x_ref[...] + _dot(merged.astype(BF16), wo_ref[...])


def _merge(x, o_nsa, h_m, proj, wa, wb, wo, *, tm=256):
    n, d = x.shape
    assert n % tm == 0 and _P_MERGE == 0
    const = lambda shape: pl.BlockSpec(shape, lambda i: (0, 0), pipeline_mode=pl.Buffered(1))
    return pl.pallas_call(
        _merge_kernel,
        grid=(n // tm,),
        in_specs=[
            pl.BlockSpec((tm, d), lambda i: (i, 0)),
            pl.BlockSpec((tm, o_nsa.shape[1]), lambda i: (i, 0)),
            pl.BlockSpec((tm, h_m.shape[1]), lambda i: (i, 0)),
            pl.BlockSpec((tm, d), lambda i: (i, 0)),
            pl.BlockSpec((tm, d), lambda i: (i, 1)),
            const(wa.shape), const(wb.shape), const(wo.shape),
        ],
        out_specs=pl.BlockSpec((tm, d), lambda i: (i, 0)),
        out_shape=jax.ShapeDtypeStruct((n, d), F32),
        compiler_params=_params("parallel"),
        name="merge",
    )(x, o_nsa, h_m, proj, proj, wa, wb, wo)


def _layer(x2, b, s, p):
    n = x2.shape[0]
    g, r, dh = NSA_KV_HEADS, NSA_GROUP, NSA_HEAD_DIM
    w_in, b_in = p["w_in"], p["b_in"]

    def cols(a, start, size):
        return a[..., start:start + size]

    order = [(_OFF_MERGE, 2 * D_MODEL), (_OFF_Q, NSA_Q_WIDTH), (_OFF_MQKV, 3 * M_WIDTH),
             (_OFF_MO, M_WIDTH), (_OFF_KV, 6 * NSA_KV_WIDTH)]
    w_main = jnp.concatenate([cols(w_in, o, z) for o, z in order], axis=1).astype(BF16)
    b_main = jnp.concatenate([cols(b_in, o, z) for o, z in order], axis=0)[None, :]
    small_order = [(_OFF_G, 3 * NSA_HEADS), (_OFF_MI, M_HEADS), (_OFF_MF, M_HEADS)]
    n_small = sum(z for _, z in small_order)
    w_small = jnp.pad(jnp.concatenate([cols(w_in, o, z) for o, z in small_order], axis=1),
                      ((0, 0), (0, _SMALL_WIDTH - n_small))).astype(BF16)
    b_small = jnp.pad(jnp.concatenate([cols(b_in, o, z) for o, z in small_order], axis=0),
                      (0, _SMALL_WIDTH - n_small))[None, :]

    x1, hn, small = _ffn(x2, p["ffn1_norm"][None, :], p["ffn1_w_gate"].astype(BF16),
                         p["ffn1_w_up"].astype(BF16), p["ffn1_w_down"].astype(BF16),
                         mix=(p["mix_norm"][None, :], w_small, b_small))
    proj = _proj(hn, w_main, b_main)

    tq = _ATTN_TQ
    nq = s // tq
    qt = proj[:, _P_Q:_P_Q + NSA_Q_WIDTH].reshape(b, nq, tq, g, r, dh).transpose(0, 3, 1, 5, 4, 2)
    qt = qt.reshape(b, g, nq, dh, r * tq)
    gates_t = small[:, 0:3 * NSA_HEADS].reshape(b, nq, tq, g, r, 3).transpose(0, 3, 1, 5, 4, 2)
    gates_t = gates_t.reshape(b, g, nq, 3, r * tq)
    kv6 = proj[:, _P_KV:_P_KV + 6 * NSA_KV_WIDTH].reshape(b, s, 6, g, dh)
    rows = s // CMP_STRIDE
    blk_w = CMP_STRIDE * dh
    seq_major = lambda j: kv6[:, :, j].transpose(0, 2, 1, 3).reshape(b * g, s, dh)
    feat_major = lambda j: kv6[:, :, j].transpose(0, 2, 3, 1).reshape(b * g, dh, s)
    kc, vc, ksa, kwn = _prep(
        seq_major(0).reshape(b * g, rows, blk_w), seq_major(1).reshape(b * g, rows, blk_w),
        seq_major(2), seq_major(4),
        p["cmp_pos_k"].reshape(1, CMP_BLOCK * dh), p["cmp_w1_k"].astype(BF16), p["cmp_w2_k"].astype(BF16),
        p["cmp_pos_v"].reshape(1, CMP_BLOCK * dh), p["cmp_w1_v"].astype(BF16), p["cmp_w2_v"].astype(BF16),
        p["nsa_kc_gain"][None, :], p["nsa_ks_gain"][None, :], p["nsa_kw_gain"][None, :])
    c_start = np.arange(rows) * CMP_STRIDE
    b_start = np.arange(dh) * SLC_BLOCK
    ovt = ((c_start[None, :] < (b_start + SLC_BLOCK)[:, None]) & ((c_start + CMP_BLOCK)[None, :] > b_start[:, None])
           & (np.arange(rows) < rows - 1)[None, :])
    o_t = _attn(qt, gates_t, kc, vc.transpose(0, 2, 1), ksa, feat_major(3), kwn, feat_major(5),
                p["nsa_q_gain"][:, None], jnp.asarray(ovt, BF16), tq=tq)
    o_nsa = o_t.reshape(b, g, nq, dh, r, tq).transpose(0, 2, 5, 1, 4, 3).reshape(n, NSA_Q_WIDTH)

    gate_t = small[:, _S_MI:_S_MI + 2 * M_HEADS].reshape(b, s // M_CHUNK, M_CHUNK, 2 * M_HEADS)
    gate_t = gate_t.transpose(0, 1, 3, 2)
    upper = np.triu(np.ones((M_CHUNK, M_CHUNK), np.float32))
    cum_u = jnp.asarray(np.stack([upper, upper.T]), BF16)
    h_m = _mlstm(proj, small, gate_t, p["m_conv_w"], p["m_conv_b"][None, :],
                 p["m_out_gain"].reshape(1, M_WIDTH), cum_u, batch=b)

    x3 = _merge(x1, o_nsa, h_m, proj, p["w_branch_nsa"].astype(BF16), p["w_branch_mlstm"].astype(BF16),
                p["w_out"].astype(BF16))
    return _ffn(x3, p["ffn2_norm"][None, :], p["ffn2_w_gate"].astype(BF16),
                p["ffn2_w_up"].astype(BF16), p["ffn2_w_down"].astype(BF16))


def kernel(x, ffn1_norm, ffn1_w_gate, ffn1_w_up, ffn1_w_down, mix_norm, w_in, b_in, nsa_q_gain, nsa_kc_gain, nsa_ks_gain, nsa_kw_gain, cmp_pos_k, cmp_w1_k, cmp_w2_k, cmp_pos_v, cmp_w1_v, cmp_w2_v, m_conv_w, m_conv_b, m_out_gain, w_branch_nsa, w_branch_mlstm, w_out, ffn2_norm, ffn2_w_gate, ffn2_w_up, ffn2_w_down):
    params = dict(ffn1_norm=ffn1_norm, ffn1_w_gate=ffn1_w_gate, ffn1_w_up=ffn1_w_up, ffn1_w_down=ffn1_w_down,
                  mix_norm=mix_norm, w_in=w_in, b_in=b_in, nsa_q_gain=nsa_q_gain, nsa_kc_gain=nsa_kc_gain,
                  nsa_ks_gain=nsa_ks_gain, nsa_kw_gain=nsa_kw_gain, cmp_pos_k=cmp_pos_k, cmp_w1_k=cmp_w1_k,
                  cmp_w2_k=cmp_w2_k, cmp_pos_v=cmp_pos_v, cmp_w1_v=cmp_w1_v, cmp_w2_v=cmp_w2_v,
                  m_conv_w=m_conv_w, m_conv_b=m_conv_b, m_out_gain=m_out_gain, w_branch_nsa=w_branch_nsa,
                  w_branch_mlstm=w_branch_mlstm, w_out=w_out, ffn2_norm=ffn2_norm, ffn2_w_gate=ffn2_w_gate,
                  ffn2_w_up=ffn2_w_up, ffn2_w_down=ffn2_w_down)
    b, s, d = x.shape
    h = x.reshape(b * s, d)
    for l in range(ffn1_norm.shape[0]):
        h = _layer(h, b, s, {k: v[l] for k, v in params.items()})
    return h.reshape(b, s, d)
```

```python
import functools

import numpy as np
import jax
import jax.numpy as jnp
from jax import lax
from jax.experimental import pallas as pl
from jax.experimental.pallas import tpu as pltpu

F32 = jnp.float32
BF16 = jnp.bfloat16

D_MODEL = 2048
D_FF = 5632
NSA_HEADS = 16
NSA_KV_HEADS = 4
NSA_GROUP = NSA_HEADS // NSA_KV_HEADS
NSA_HEAD_DIM = 64
NSA_Q_WIDTH = NSA_HEADS * NSA_HEAD_DIM
NSA_KV_WIDTH = NSA_KV_HEADS * NSA_HEAD_DIM
CMP_BLOCK = 32
CMP_STRIDE = 16
CMP_HIDDEN = 256
SLC_BLOCK = 64
SLC_TOPN = 16
WINDOW = 512
M_HEADS = 4
M_HEAD_DIM = 256
M_WIDTH = M_HEADS * M_HEAD_DIM
M_CHUNK = 64
CONV_WIDTH = 4
EPS = 1e-6
NEG_INF = -1e30
FORCE_SCORE = 1e4
LOG2_E = 1.4426950408889634

_OFF_Q = 0
_OFF_KV = _OFF_Q + NSA_Q_WIDTH
_OFF_G = _OFF_KV + 6 * NSA_KV_WIDTH
_OFF_MQKV = _OFF_G + 3 * NSA_HEADS
_OFF_MI = _OFF_MQKV + 3 * M_WIDTH
_OFF_MF = _OFF_MI + M_HEADS
_OFF_MO = _OFF_MF + M_HEADS
_OFF_MERGE = _OFF_MO + M_WIDTH
_IN_WIDTH = _OFF_MERGE + 2 * D_MODEL

_P_MERGE = 0
_P_Q = 2 * D_MODEL
_P_MQ = _P_Q + NSA_Q_WIDTH
_P_MO = _P_MQ + 3 * M_WIDTH
_P_KV = _P_MO + M_WIDTH
_P_WIDTH = _P_KV + 6 * NSA_KV_WIDTH
_SMALL_WIDTH = 128
_GATE_SLOT = 16
_S_MI = NSA_KV_HEADS * _GATE_SLOT
_S_MF = _S_MI + M_HEADS

VMEM_LIMIT_BYTES = 60000 * 1024
_ATTN_TQ = 256
_ATTN_PAIR = 2

_NT = (((1,), (1,)), ((), ()))
_TN = (((0,), (0,)), ((), ()))


def _params(*sem):
    return pltpu.CompilerParams(dimension_semantics=sem, vmem_limit_bytes=VMEM_LIMIT_BYTES)


def _dot(a, b):
    return jnp.dot(a, b, preferred_element_type=F32)


def _split3(x):
    hi = x.astype(BF16)
    r1 = x - hi.astype(F32)
    mid = r1.astype(BF16)
    lo = (r1 - mid.astype(F32)).astype(BF16)
    return hi, mid, lo


def _rms(x, gain):
    return x * lax.rsqrt(jnp.mean(x * x, axis=-1, keepdims=True) + EPS) * gain


def _ffn_kernel(*refs, n_ff, emit_mix):
    if emit_mix:
        (x_ref, g_ref, wg_ref, wu_ref, wd_ref, g2_ref, ws_ref, bs_ref,
         o_ref, hn2_ref, small_ref, hn_ref) = refs
    else:
        x_ref, g_ref, wg_ref, wu_ref, wd_ref, o_ref, hn_ref = refs
    j = pl.program_id(1)

    @pl.when(j == 0)
    def _():
        hn_ref[...] = _rms(x_ref[...], g_ref[...]).astype(BF16)
        o_ref[...] = jnp.zeros_like(o_ref)

    hn = hn_ref[...]
    a = _dot(hn, wg_ref[...])
    b = _dot(hn, wu_ref[...])
    h = (a * jax.nn.sigmoid(a) * b).astype(BF16)
    o_ref[...] += _dot(h, wd_ref[...])

    @pl.when(j == n_ff - 1)
    def _():
        y = x_ref[...] + 0.5 * o_ref[...]
        o_ref[...] = y
        if emit_mix:
            hn2 = _rms(y, g2_ref[...]).astype(BF16)
            hn2_ref[...] = hn2
            small_ref[...] = _dot(hn2, ws_ref[...]) + bs_ref[...]


def _ffn(x, gain, wg, wu, wd, mix=None, *, tm=1024, tf=512):
    n, d = x.shape
    f = wg.shape[1]
    assert n % tm == 0 and f % tf == 0
    n_ff = f // tf
    in_specs = [
        pl.BlockSpec((tm, d), lambda i, j: (i, 0), pipeline_mode=pl.Buffered(1)),
        pl.BlockSpec((1, d), lambda i, j: (0, 0)),
        pl.BlockSpec((d, tf), lambda i, j: (0, j)),
        pl.BlockSpec((d, tf), lambda i, j: (0, j)),
        pl.BlockSpec((tf, d), lambda i, j: (j, 0)),
    ]
    out_shape = [jax.ShapeDtypeStruct((n, d), F32)]
    out_specs = [pl.BlockSpec((tm, d), lambda i, j: (i, 0))]
    args = [x, gain, wg, wu, wd]
    if mix is not None:
        g2, ws, bs = mix
        in_specs += [
            pl.BlockSpec((1, d), lambda i, j: (0, 0)),
            pl.BlockSpec((d, _SMALL_WIDTH), lambda i, j: (0, 0)),
            pl.BlockSpec((1, _SMALL_WIDTH), lambda i, j: (0, 0)),
        ]
        out_shape += [jax.ShapeDtypeStruct((n, d), BF16), jax.ShapeDtypeStruct((n, _SMALL_WIDTH), F32)]
        out_specs += [pl.BlockSpec((tm, d), lambda i, j: (i, 0)),
                      pl.BlockSpec((tm, _SMALL_WIDTH), lambda i, j: (i, 0))]
        args += [g2, ws, bs]
    res = pl.pallas_call(
        functools.partial(_ffn_kernel, n_ff=n_ff, emit_mix=mix is not None),
        grid=(n // tm, n_ff),
        in_specs=in_specs,
        out_specs=out_specs,
        out_shape=out_shape,
        scratch_shapes=[pltpu.VMEM((tm, d), BF16)],
        compiler_params=_params("parallel", "arbitrary"),
        name="ffn_mix" if mix is not None else "ffn",
    )(*args)
    return res if mix is not None else res[0]


def _proj_kernel(a_ref, w_ref, b_ref, o_ref):
    o_ref[...] = (_dot(a_ref[...], w_ref[...]) + b_ref[...]).astype(o_ref.dtype)


def _proj(a, w, b, *, tm=1024, tn=1536):
    n, d = a.shape
    p = w.shape[1]
    assert n % tm == 0 and p % tn == 0
    return pl.pallas_call(
        _proj_kernel,
        grid=(p // tn, n // tm),
        in_specs=[
            pl.BlockSpec((tm, d), lambda j, i: (i, 0)),
            pl.BlockSpec((d, tn), lambda j, i: (0, j)),
            pl.BlockSpec((1, tn), lambda j, i: (0, j)),
        ],
        out_specs=pl.BlockSpec((tm, tn), lambda j, i: (i, j)),
        out_shape=jax.ShapeDtypeStruct((n, p), BF16),
        compiler_params=_params("parallel", "parallel"),
        name="proj",
    )(a, w, b)


def _prep_kernel(uk_ref, uv_ref, kv_ref, pk_ref, w1k_ref, w2k_ref, pv_ref, w1v_ref, w2vt_ref,
                 gc_ref, gs_ref, gw_ref, kc_ref, vct_ref, ksa_ref, vst_ref, kwn_ref, vwt_ref):
    half = w1k_ref.shape[0] // 2
    rows = uk_ref.shape[1]
    dh = NSA_HEAD_DIM

    def hidden(u_ref, pos_ref, w1_ref):
        u = u_ref[0]
        top = _dot(u, w1_ref[:half, :])
        bot = _dot(u, w1_ref[half:, :])
        pos = jnp.broadcast_to(pos_ref[...], (8, pos_ref.shape[1])).astype(BF16)
        pb = _dot(pos, w1_ref[...])[0:1, :]
        return jax.nn.gelu(top + pltpu.roll(bot, rows - 1, axis=0) + pb).astype(BF16)

    kc = _dot(hidden(uk_ref, pk_ref, w1k_ref), w2k_ref[...])
    kc_ref[0] = _rms(kc, gc_ref[...]).astype(BF16)
    vct_ref[0] = lax.dot_general(w2vt_ref[...], hidden(uv_ref, pv_ref, w1v_ref), _NT,
                                 preferred_element_type=F32).astype(BF16)

    s = kv_ref.shape[0]
    slc = kv_ref[:, 2 * dh:4 * dh].astype(F32)
    ksn = _rms(slc[:, 0:dh], gs_ref[...]).astype(BF16)
    key_blk = lax.broadcasted_iota(jnp.int32, (s, dh), 0) // SLC_BLOCK
    blk = lax.broadcasted_iota(jnp.int32, (s, dh), 1)
    onehot = (key_blk == blk).astype(BF16)
    ksa_ref[0] = jnp.concatenate([ksn, onehot], axis=1)
    vst_ref[0] = slc.T[dh:2 * dh, :].astype(BF16)
    win = kv_ref[:, 4 * dh:6 * dh].astype(F32)
    kwn_ref[0] = _rms(win[:, 0:dh], gw_ref[...]).astype(BF16)
    vwt_ref[0] = win.T[dh:2 * dh, :].astype(BF16)


def _prep(uk, uv, proj, pos_k, w1k, w2k, pos_v, w1v, w2vt, gc, gs, gw, *, batch, groups):
    bg, rows, width = uk.shape
    s = proj.shape[0] // batch
    dh = NSA_HEAD_DIM
    hid = w1k.shape[1]
    kv_w = 6 * dh
    kv_col = _P_KV // kv_w
    assert kv_col * kv_w == _P_KV and bg == batch * groups
    full = lambda shape: pl.BlockSpec(shape, lambda i: (0,) * len(shape))
    per = lambda shape: pl.BlockSpec((1,) + shape, lambda i: (i, 0, 0))
    return pl.pallas_call(
        _prep_kernel,
        grid=(bg,),
        in_specs=[per((rows, width)), per((rows, width)),
                  pl.BlockSpec((s, kv_w), lambda i: (i // groups, kv_col + i % groups)),
                  full((1, 2 * width)), full((2 * width, hid)), full((hid, dh)),
                  full((1, 2 * width)), full((2 * width, hid)), full((dh, hid)),
                  full((1, dh)), full((1, dh)), full((1, dh))],
        out_specs=[per((rows, dh)), per((dh, rows)), per((s, 2 * dh)), per((dh, s)), per((s, dh)), per((dh, s))],
        out_shape=[jax.ShapeDtypeStruct((bg, rows, dh), BF16), jax.ShapeDtypeStruct((bg, dh, rows), BF16),
                   jax.ShapeDtypeStruct((bg, s, 2 * dh), BF16), jax.ShapeDtypeStruct((bg, dh, s), BF16),
                   jax.ShapeDtypeStruct((bg, s, dh), BF16), jax.ShapeDtypeStruct((bg, dh, s), BF16)],
        compiler_params=_params("parallel"),
        name="nsa_prep",
    )(uk, uv, proj, pos_k, w1k, w2k, pos_v, w1v, w2vt, gc, gs, gw)


def _attn_kernel(q_ref, small_ref, kc_ref, vct_ref, ksa_ref, vst_ref, kw_ref, vwt_ref, qg_ref, ovt_ref,
                 o_ref, qa_ref, m_ref, l_ref, acc_ref, ocmp_ref, score_ref, rank_ref, gate_ref,
                 *, tq, pair, groups):
    r = NSA_GROUP
    dh = NSA_HEAD_DIM
    n_cmp = kc_ref.shape[1]
    i = pl.program_id(1)
    q0 = i * tq
    head_lanes = [slice(h * tq, (h + 1) * tq) for h in range(r)]

    q_t = q_ref[...].astype(F32).T
    for h, lanes in enumerate(head_lanes):
        q = q_t[h * dh:(h + 1) * dh, :]
        inv = lax.rsqrt(jnp.mean(q * q, axis=0, keepdims=True) + EPS)
        qa_ref[0:dh, lanes] = (q * inv * qg_ref[...] * (dh ** -0.5 * LOG2_E)).astype(BF16)
    gate_ref[...] = small_ref[...].T
    slot = pl.multiple_of((pl.program_id(0) % groups) * _GATE_SLOT, _GATE_SLOT)
    gt = jax.nn.sigmoid(gate_ref[pl.ds(slot, _GATE_SLOT), :])

    tq_row = q0 + lax.broadcasted_iota(jnp.int32, (1, tq), 1)

    c_end = lax.broadcasted_iota(jnp.int32, (n_cmp, 1), 0) * CMP_STRIDE + (CMP_BLOCK - 1)
    cmask = c_end <= tq_row
    has_cmp = tq_row >= CMP_BLOCK - 1
    scs = [jnp.where(cmask, _dot(kc_ref[0], qa_ref[0:dh, lanes]), NEG_INF) for lanes in head_lanes]
    es = [jnp.exp2(sc - jnp.max(sc, axis=0, keepdims=True)) for sc in scs]
    p_cmps = [e * jnp.where(has_cmp, 1.0 / jnp.sum(e, axis=0, keepdims=True), 0.0) for e in es]
    o_cmps = [_dot(vct_ref[0], p_cmp.astype(BF16)) for p_cmp in p_cmps]
    for lanes, o_cmp in zip(head_lanes, o_cmps):
        ocmp_ref[:, lanes] = o_cmp
    p_sum = functools.reduce(jnp.add, p_cmps)
    ovt = ovt_ref[...]
    imp_t = sum(_dot(ovt, part) for part in _split3(p_sum))
    n_blk = imp_t.shape[0]
    blk = lax.broadcasted_iota(jnp.int32, (n_blk, 1), 0)
    cur = tq_row // SLC_BLOCK
    forced = (blk == 0) | (blk == cur) | (blk == cur - 1)
    score_ref[...] = jnp.where(forced, FORCE_SCORE, jnp.where(blk * SLC_BLOCK <= tq_row, imp_t, -1.0))
    rank_ref[...] = jnp.zeros_like(rank_ref)
    sub = 8
    last_blk = (q0 + tq - 1) // SLC_BLOCK
    for kg in range(0, n_blk, sub):
        @pl.when(kg <= last_blk)
        def _(kg=kg):
            for ng in range(0, n_blk, sub):
                sn = score_ref[ng:ng + sub, :]
                rn = rank_ref[ng:ng + sub, :]
                for k in range(kg, kg + sub):
                    sk = score_ref[k:k + 1, :]
                    if ng > kg:
                        ahead = sk >= sn
                    elif ng < kg:
                        ahead = sk > sn
                    else:
                        later = lax.broadcasted_iota(jnp.int32, (sub, 1), 0) > (k - kg)
                        ahead = (sk > sn) | (later & (sk == sn))
                    rn = jnp.where(ahead, rn + 1.0, rn)
                rank_ref[ng:ng + sub, :] = rn
    sel_bias = jnp.where(rank_ref[...] < SLC_TOPN, 0.0, NEG_INF).astype(BF16)
    for lanes in head_lanes:
        qa_ref[dh:2 * dh, lanes] = sel_bias

    def key_block(k_ref, vt_ref, kdim, k0, n_keys, window, init):
        start = pl.multiple_of(k0, tq)
        k_blk = k_ref[0, pl.ds(start, n_keys), :]
        vt = vt_ref[0, :, pl.ds(start, n_keys)]
        ss = [_dot(k_blk, qa_ref[0:kdim, lanes]) for lanes in head_lanes]
        if window is not None:
            back = tq_row - (start + lax.broadcasted_iota(jnp.int32, (n_keys, 1), 0))
            mask = back >= 0
            if window != float("inf"):
                mask = mask & (back < window)
            ss = [jnp.where(mask, s, NEG_INF) for s in ss]
        mxs = [jnp.max(s, axis=0, keepdims=True) for s in ss]
        if init:
            ps = [jnp.exp2(s - mx) for s, mx in zip(ss, mxs)]
            for lanes, mx, p in zip(head_lanes, mxs, ps):
                m_ref[:, lanes] = mx
                l_ref[:, lanes] = jnp.sum(p, axis=0, keepdims=True)
                acc_ref[:, lanes] = _dot(vt, p.astype(BF16))
        else:
            m_prevs = [m_ref[:, lanes] for lanes in head_lanes]
            m_news = [jnp.maximum(mp, mx) for mp, mx in zip(m_prevs, mxs)]
            alphas = [jnp.exp2(mp - mn) for mp, mn in zip(m_prevs, m_news)]
            ps = [jnp.exp2(s - mn) for s, mn in zip(ss, m_news)]
            pvs = [_dot(vt, p.astype(BF16)) for p in ps]
            for lanes, mn, al, p, pv in zip(head_lanes, m_news, alphas, ps, pvs):
                m_ref[:, lanes] = mn
                l_ref[:, lanes] = al * l_ref[:, lanes] + jnp.sum(p, axis=0, keepdims=True)
                acc_ref[:, lanes] = al * acc_ref[:, lanes] + pv

    key_block(ksa_ref, vst_ref, 2 * dh, q0, tq, float("inf"), True)

    def slc_body(kt, carry):
        key_block(ksa_ref, vst_ref, 2 * dh, kt * (pair * tq), pair * tq, None, False)
        return carry

    lax.fori_loop(0, i // pair, slc_body, 0)
    for rem in range(1, pair):
        @pl.when(i % pair >= rem)
        def _(rem=rem):
            key_block(ksa_ref, vst_ref, 2 * dh, (i - rem) * tq, tq, None, False)
    for h, lanes in enumerate(head_lanes):
        ocmp_ref[:, lanes] = (gt[3 * h:3 * h + 1, :] * ocmp_ref[:, lanes]
                              + gt[3 * h + 1:3 * h + 2, :] * (acc_ref[:, lanes] / l_ref[:, lanes]))

    key_block(kw_ref, vwt_ref, dh, jnp.maximum(q0 - WINDOW, 0), WINDOW + tq, WINDOW, True)
    outs = [ocmp_ref[:, lanes] + gt[3 * h + 2:3 * h + 3, :] * (acc_ref[:, lanes] / l_ref[:, lanes])
            for h, lanes in enumerate(head_lanes)]
    o_ref[...] = jnp.concatenate(outs, axis=0).T.astype(o_ref.dtype)


def _attn(proj, small, kc, vct, ksa, vst, kwn, vwt, q_gain_col, ovt, *, batch, groups, tq):
    n = proj.shape[0]
    s = n // batch
    nq = s // tq
    r, dh = NSA_GROUP, NSA_HEAD_DIM
    m_cols = r * tq
    q_col = _P_Q // (r * dh)
    assert q_col * r * dh == _P_Q and s % tq == 0
    assert WINDOW % tq == 0 and s >= WINDOW + tq and s // SLC_BLOCK <= dh and tq % SLC_BLOCK == 0
    assert groups * _GATE_SLOT <= _SMALL_WIDTH and 3 * r <= _GATE_SLOT
    n_cmp = kc.shape[1]
    per_bg = lambda shape: pl.BlockSpec((1,) + shape, lambda bg, i: (bg, 0, 0))
    return pl.pallas_call(
        functools.partial(_attn_kernel, tq=tq, pair=_ATTN_PAIR, groups=groups),
        grid=(batch * groups, nq),
        in_specs=[
            pl.BlockSpec((tq, r * dh), lambda bg, i: ((bg // groups) * nq + i, q_col + bg % groups)),
            pl.BlockSpec((tq, _SMALL_WIDTH), lambda bg, i: ((bg // groups) * nq + i, 0)),
            per_bg((n_cmp, dh)), per_bg((dh, n_cmp)),
            per_bg((s, 2 * dh)), per_bg((dh, s)), per_bg((s, dh)), per_bg((dh, s)),
            pl.BlockSpec((dh, 1), lambda bg, i: (0, 0)),
            pl.BlockSpec(ovt.shape, lambda bg, i: (0, 0)),
        ],
        out_specs=pl.BlockSpec((tq, r * dh), lambda bg, i: ((bg // groups) * nq + i, bg % groups)),
        out_shape=jax.ShapeDtypeStruct((n, groups * r * dh), BF16),
        scratch_shapes=[pltpu.VMEM((2 * dh, m_cols), BF16),
                        pltpu.VMEM((1, m_cols), F32), pltpu.VMEM((1, m_cols), F32),
                        pltpu.VMEM((dh, m_cols), F32), pltpu.VMEM((dh, m_cols), F32),
                        pltpu.VMEM((dh, tq), F32), pltpu.VMEM((dh, tq), F32),
                        pltpu.VMEM((_SMALL_WIDTH, tq), F32)],
        compiler_params=_params("parallel", "parallel"),
        name="nsa_attn",
    )(proj, small, kc, vct, ksa, vst, kwn, vwt, q_gain_col, ovt)


def _mlstm_kernel(mq_ref, mk_ref, mv_ref, mo_ref, small_ref, gt_ref, cw_ref, cb_ref, og_ref, cum_ref,
                  o_ref, xbuf_ref, c_ref, n_ref, m_ref, *, tt):
    hd = M_HEAD_DIM
    L = M_CHUNK
    width = M_WIDTH
    n_chunk = tt // L
    j = pl.program_id(1)

    @pl.when(j == 0)
    def _():
        xbuf_ref[0:8, :] = jnp.zeros((8, 2 * width), F32)
        c_ref[...] = jnp.zeros_like(c_ref)
        n_ref[...] = jnp.zeros_like(n_ref)
        m_ref[...] = jnp.zeros_like(m_ref)

    @pl.when(j > 0)
    def _():
        xbuf_ref[0:8, :] = xbuf_ref[tt:tt + 8, :]

    xbuf_ref[8:8 + tt, 0:width] = mq_ref[...].astype(F32)
    xbuf_ref[8:8 + tt, width:2 * width] = mk_ref[...].astype(F32)
    conv = cb_ref[...]
    for tap in range(CONV_WIDTH):
        off = 8 - (CONV_WIDTH - 1) + tap
        conv = conv + cw_ref[tap:tap + 1, :] * xbuf_ref[off:off + tt, :]
    qk = conv * jax.nn.sigmoid(conv)
    q_all = qk[:, 0:width]
    k_all = qk[:, width:2 * width] * (hd ** -0.5)

    small = small_ref[...]
    logf_col = jax.nn.log_sigmoid(small)
    gt = gt_ref[0].reshape(n_chunk * 8, L)
    logf_row = jax.nn.log_sigmoid(gt)
    cum_u = cum_ref[0]
    cum_l = cum_ref[1]
    a_row_all = sum(_dot(part, cum_u) for part in _split3(logf_row))
    causal = lax.broadcasted_iota(jnp.int32, (L, L), 1) <= lax.broadcasted_iota(jnp.int32, (L, L), 0)

    for c in range(n_chunk):
        rows = slice(c * L, (c + 1) * L)
        a_col_c = sum(_dot(cum_l, part) for part in _split3(logf_col[rows]))
        pending = []
        for h in range(M_HEADS):
            lanes = slice(h * hd, (h + 1) * hd)
            q_c = q_all[rows, lanes]
            k_c = k_all[rows, lanes]
            v_c = mv_ref[rows, lanes]
            q_b = q_c.astype(BF16)
            a_j = a_col_c[:, _S_MF + h:_S_MF + h + 1]
            li_j = small[rows, _S_MI + h:_S_MI + h + 1]
            a_s = a_row_all[c * 8 + M_HEADS + h:c * 8 + M_HEADS + h + 1, :]
            li_s = gt[c * 8 + h:c * 8 + h + 1, :]
            g_c = a_j[L - 1:L, :]
            m_st = m_ref[h:h + 1, 0:1]
            c_st = c_ref[h]
            n_st = n_ref[h]

            log_w = jnp.where(causal, a_j - a_s + li_s, NEG_INF)
            m_intra = jnp.max(log_w, axis=-1, keepdims=True)
            w = jnp.exp(log_w - m_intra) * lax.dot_general(q_b, k_c.astype(BF16), _NT, preferred_element_type=F32)
            num_intra = _dot(w.astype(BF16), v_c)
            den_intra = jnp.sum(w, axis=-1, keepdims=True)

            num_inter = _dot(q_b, c_st.astype(BF16))
            den_inter = jnp.sum(q_c * n_st, axis=-1, keepdims=True)

            log_inter = a_j + m_st
            m_comb = jnp.maximum(log_inter, m_intra)
            s_inter = jnp.exp(log_inter - m_comb)
            s_intra = jnp.exp(m_intra - m_comb)
            num = s_inter * num_inter + s_intra * num_intra
            den = s_inter * den_inter + s_intra * den_intra
            hcell = num / jnp.maximum(jnp.abs(den), jnp.exp(-m_comb))

            log_u = g_c - a_j + li_j
            m_new = jnp.maximum(g_c + m_st, jnp.max(log_u, axis=0, keepdims=True))
            decay = jnp.exp(g_c + m_st - m_new)
            uk = jnp.exp(log_u - m_new) * k_c
            c_new = decay * c_st + lax.dot_general(uk.astype(BF16), v_c, _TN, preferred_element_type=F32)
            n_new = decay * n_st + jnp.sum(uk, axis=0, keepdims=True)

            hn = _rms(hcell, og_ref[:, lanes])
            o_gate = jax.nn.sigmoid(mo_ref[rows, lanes].astype(F32))
            pending.append((h, lanes, c_new, n_new, m_new, (o_gate * hn).astype(o_ref.dtype)))
        for h, lanes, c_new, n_new, m_new, out in pending:
            c_ref[h] = c_new
            n_ref[h] = n_new
            m_ref[h:h + 1, :] = jnp.broadcast_to(m_new, (1, m_ref.shape[1]))
            o_ref[rows, lanes] = out


def _mlstm(proj, small, gate_t, conv_w, conv_b, out_gain, cum_u, *, batch, tt=256):
    n = proj.shape[0]
    s = n // batch
    assert s % tt == 0 and tt % M_CHUNK == 0
    nt = s // tt
    cq = _P_MQ // M_WIDTH
    assert cq * M_WIDTH == _P_MQ
    col = lambda k: pl.BlockSpec((tt, M_WIDTH), lambda b, j: (b * nt + j, k))
    full = lambda shape: pl.BlockSpec(shape, lambda b, j: (0,) * len(shape))
    return pl.pallas_call(
        functools.partial(_mlstm_kernel, tt=tt),
        grid=(batch, nt),
        in_specs=[col(cq), col(cq + 1), col(cq + 2), col(cq + 3),
                  pl.BlockSpec((tt, _SMALL_WIDTH), lambda b, j: (b * nt + j, 0)),
                  pl.BlockSpec((1, tt // M_CHUNK, 8, M_CHUNK), lambda b, j: (b, j, 0, 0)),
                  full(conv_w.shape), full(conv_b.shape), full(out_gain.shape), full(cum_u.shape)],
        out_specs=pl.BlockSpec((tt, M_WIDTH), lambda b, j: (b * nt + j, 0)),
        out_shape=jax.ShapeDtypeStruct((n, M_WIDTH), BF16),
        scratch_shapes=[pltpu.VMEM((tt + 8, 2 * M_WIDTH), F32),
                        pltpu.VMEM((M_HEADS, M_HEAD_DIM, M_HEAD_DIM), F32),
                        pltpu.VMEM((M_HEADS, 1, M_HEAD_DIM), F32),
                        pltpu.VMEM((8, 128), F32)],
        compiler_params=_params("parallel", "arbitrary"),
        name="mlstm",
    )(proj, proj, proj, proj, small, gate_t, conv_w, conv_b, out_gain, cum_u)


def _merge_kernel(x_ref, oa_ref, ob_ref, ga_ref, gb_ref, wa_ref, wb_ref, wo_ref, o_ref):
    ya = _dot(oa_ref[...], wa_ref[...])
    yb = _dot(ob_ref[...], wb_ref[...])
    merged = (jax.nn.sigmoid(ga_ref[...].astype(F32)) * ya
              + jax.nn.sigmoid(gb_ref[...].astype(F32)) * yb)
    o_ref[...] = x_ref[...] + _dot(merged.astype(BF16), wo_ref[...])


def _merge(x, o_nsa, h_m, proj, wa, wb, wo, *, tm=256):
    n, d = x.shape
    assert n % tm == 0 and _P_MERGE == 0
    const = lambda shape: pl.BlockSpec(shape, lambda i: (0, 0), pipeline_mode=pl.Buffered(1))
    return pl.pallas_call(
        _merge_kernel,
        grid=(n // tm,),
        in_specs=[
            pl.BlockSpec((tm, d), lambda i: (i, 0)),
            pl.BlockSpec((tm, o_nsa.shape[1]), lambda i: (i, 0)),
            pl.BlockSpec((tm, h_m.shape[1]), lambda i: (i, 0)),
            pl.BlockSpec((tm, d), lambda i: (i, 0)),
            pl.BlockSpec((tm, d), lambda i: (i, 1)),
            const(wa.shape), const(wb.shape), const(wo.shape),
        ],
        out_specs=pl.BlockSpec((tm, d), lambda i: (i, 0)),
        out_shape=jax.ShapeDtypeStruct((n, d), F32),
        compiler_params=_params("parallel"),
        name="merge",
    )(x, o_nsa, h_m, proj, proj, wa, wb, wo)


def _layer(x2, b, s, p):
    n = x2.shape[0]
    g, r, dh = NSA_KV_HEADS, NSA_GROUP, NSA_HEAD_DIM
    w_in, b_in = p["w_in"], p["b_in"]

    def cols(a, start, size):
        return a[..., start:start + size]

    def kv_group_major(a):
        kv = cols(a, _OFF_KV, 6 * NSA_KV_WIDTH)
        kv = kv.reshape(a.shape[:-1] + (6, g, dh))
        return jnp.swapaxes(kv, -3, -2).reshape(a.shape[:-1] + (6 * NSA_KV_WIDTH,))

    def gate_slots(a):
        gates = cols(a, _OFF_G, 3 * NSA_HEADS).reshape(a.shape[:-1] + (g, 3 * r))
        pad = [(0, 0)] * (gates.ndim - 1) + [(0, _GATE_SLOT - 3 * r)]
        return jnp.pad(gates, pad).reshape(a.shape[:-1] + (g * _GATE_SLOT,))

    def main_cols(a):
        return jnp.concatenate([cols(a, _OFF_MERGE, 2 * D_MODEL), cols(a, _OFF_Q, NSA_Q_WIDTH),
                                cols(a, _OFF_MQKV, 3 * M_WIDTH), cols(a, _OFF_MO, M_WIDTH),
                                kv_group_major(a)], axis=-1)

    def small_cols(a):
        used = jnp.concatenate([gate_slots(a), cols(a, _OFF_MI, M_HEADS), cols(a, _OFF_MF, M_HEADS)], axis=-1)
        pad = [(0, 0)] * (used.ndim - 1) + [(0, _SMALL_WIDTH - used.shape[-1])]
        return jnp.pad(used, pad)

    w_main = main_cols(w_in).astype(BF16)
    b_main = main_cols(b_in)[None, :]
    w_small = small_cols(w_in).astype(BF16)
    b_small = small_cols(b_in)[None, :]

    x1, hn, small = _ffn(x2, p["ffn1_norm"][None, :], p["ffn1_w_gate"].astype(BF16),
                         p["ffn1_w_up"].astype(BF16), p["ffn1_w_down"].astype(BF16),
                         mix=(p["mix_norm"][None, :], w_small, b_small))
    proj = _proj(hn, w_main, b_main)

    rows = s // CMP_STRIDE
    blk_w = CMP_STRIDE * dh
    kv_cmp = proj[:, _P_KV:_P_KV + 6 * NSA_KV_WIDTH].reshape(b, s, g, 6, dh)[:, :, :, 0:2]
    kv_cmp = kv_cmp.transpose(3, 0, 2, 1, 4).reshape(2, b * g, rows, blk_w)
    kc, vct, ksa, vst, kwn, vwt = _prep(
        kv_cmp[0], kv_cmp[1], proj,
        p["cmp_pos_k"].reshape(1, CMP_BLOCK * dh), p["cmp_w1_k"].astype(BF16), p["cmp_w2_k"].astype(BF16),
        p["cmp_pos_v"].reshape(1, CMP_BLOCK * dh), p["cmp_w1_v"].astype(BF16), p["cmp_w2_v"].T.astype(BF16),
        p["nsa_kc_gain"][None, :], p["nsa_ks_gain"][None, :], p["nsa_kw_gain"][None, :], batch=b, groups=g)
    c_start = np.arange(rows) * CMP_STRIDE
    b_start = np.arange(dh) * SLC_BLOCK
    ovt = ((c_start[None, :] < (b_start + SLC_BLOCK)[:, None]) & ((c_start + CMP_BLOCK)[None, :] > b_start[:, None])
           & (np.arange(rows) < rows - 1)[None, :])
    o_nsa = _attn(proj, small, kc, vct, ksa, vst, kwn, vwt, p["nsa_q_gain"][:, None],
                  jnp.asarray(ovt, BF16), batch=b, groups=g, tq=_ATTN_TQ)

    gate_t = small[:, _S_MI:_S_MI + 2 * M_HEADS].reshape(b, s // M_CHUNK, M_CHUNK, 2 * M_HEADS)
    gate_t = gate_t.transpose(0, 1, 3, 2)
    upper = np.triu(np.ones((M_CHUNK, M_CHUNK), np.float32))
    cum_u = jnp.asarray(np.stack([upper, upper.T]), BF16)
    h_m = _mlstm(proj, small, gate_t, p["m_conv_w"], p["m_conv_b"][None, :],
                 p["m_out_gain"].reshape(1, M_WIDTH), cum_u, batch=b)

    x3 = _merge(x1, o_nsa, h_m, proj, p["w_branch_nsa"].astype(BF16), p["w_branch_mlstm"].astype(BF16),
                p["w_out"].astype(BF16))
    return _ffn(x3, p["ffn2_norm"][None, :], p["ffn2_w_gate"].astype(BF16),
                p["ffn2_w_up"].astype(BF16), p["ffn2_w_down"].astype(BF16))


def kernel(x, ffn1_norm, ffn1_w_gate, ffn1_w_up, ffn1_w_down, mix_norm, w_in, b_in, nsa_q_gain, nsa_kc_gain, nsa_ks_gain, nsa_kw_gain, cmp_pos_k, cmp_w1_k, cmp_w2_k, cmp_pos_v, cmp_w1_v, cmp_w2_v, m_conv_w, m_conv_b, m_out_gain, w_branch_nsa, w_branch_mlstm, w_out, ffn2_norm, ffn2_w_gate, ffn2_w_up, ffn2_w_down):
    params = dict(ffn1_norm=ffn1_norm, ffn1_w_gate=ffn1_w_gate, ffn1_w_up=ffn1_w_up, ffn1_w_down=ffn1_w_down,
                  mix_norm=mix_norm, w_in=w_in, b_in=b_in, nsa_q_gain=nsa_q_gain, nsa_kc_gain=nsa_kc_gain,
                  nsa_ks_gain=nsa_ks_gain, nsa_kw_gain=nsa_kw_gain, cmp_pos_k=cmp_pos_k, cmp_w1_k=cmp_w1_k,
                  cmp_w2_k=cmp_w2_k, cmp_pos_v=cmp_pos_v, cmp_w1_v=cmp_w1_v, cmp_w2_v=cmp_w2_v,
                  m_conv_w=m_conv_w, m_conv_b=m_conv_b, m_out_gain=m_out_gain, w_branch_nsa=w_branch_nsa,
                  w_branch_mlstm=w_branch_mlstm, w_out=w_out, ffn2_norm=ffn2_norm, ffn2_w_gate=ffn2_w_gate,
                  ffn2_w_up=ffn2_w_up, ffn2_w_down=ffn2_w_down)
    b, s, d = x.shape
    h = x.reshape(b * s, d)
    for l in range(ffn1_norm.shape[0]):
        h = _layer(h, b, s, {k: v[l] for k, v in params.items()})
    return h.reshape(b, s, d)
```

```python
import functools

import numpy as np
import jax
import jax.numpy as jnp
from jax import lax
from jax.experimental import pallas as pl
from jax.experimental.pallas import tpu as pltpu

F32 = jnp.float32
BF16 = jnp.bfloat16

D_MODEL = 2048
D_FF = 5632
NSA_HEADS = 16
NSA_KV_HEADS = 4
NSA_GROUP = NSA_HEADS // NSA_KV_HEADS
NSA_HEAD_DIM = 64
NSA_Q_WIDTH = NSA_HEADS * NSA_HEAD_DIM
NSA_KV_WIDTH = NSA_KV_HEADS * NSA_HEAD_DIM
CMP_BLOCK = 32
CMP_STRIDE = 16
CMP_HIDDEN = 256
SLC_BLOCK = 64
SLC_TOPN = 16
WINDOW = 512
M_HEADS = 4
M_HEAD_DIM = 256
M_WIDTH = M_HEADS * M_HEAD_DIM
M_CHUNK = 64
CONV_WIDTH = 4
EPS = 1e-6
NEG_INF = -1e30
FORCE_SCORE = 1e4
LOG2_E = 1.4426950408889634

_OFF_Q = 0
_OFF_KV = _OFF_Q + NSA_Q_WIDTH
_OFF_G = _OFF_KV + 6 * NSA_KV_WIDTH
_OFF_MQKV = _OFF_G + 3 * NSA_HEADS
_OFF_MI = _OFF_MQKV + 3 * M_WIDTH
_OFF_MF = _OFF_MI + M_HEADS
_OFF_MO = _OFF_MF + M_HEADS
_OFF_MERGE = _OFF_MO + M_WIDTH
_IN_WIDTH = _OFF_MERGE + 2 * D_MODEL

_P_MERGE = 0
_P_Q = 2 * D_MODEL
_P_MQ = _P_Q + NSA_Q_WIDTH
_P_MO = _P_MQ + 3 * M_WIDTH
_P_KV = _P_MO + M_WIDTH
_P_WIDTH = _P_KV + 6 * NSA_KV_WIDTH
_SMALL_WIDTH = 128
_GATE_SLOT = 16
_S_MI = NSA_KV_HEADS * _GATE_SLOT
_S_MF = _S_MI + M_HEADS

VMEM_LIMIT_BYTES = 56 * 1024 * 1024
_ATTN_TQ = 256
_ATTN_PAIR = 2

_NT = (((1,), (1,)), ((), ()))
_TN = (((0,), (0,)), ((), ()))


def _params(*sem):
    return pltpu.CompilerParams(dimension_semantics=sem, vmem_limit_bytes=VMEM_LIMIT_BYTES)


def _dot(a, b):
    return jnp.dot(a, b, preferred_element_type=F32)


def _split3(x):
    hi = x.astype(BF16)
    r1 = x - hi.astype(F32)
    mid = r1.astype(BF16)
    lo = (r1 - mid.astype(F32)).astype(BF16)
    return hi, mid, lo


def _rms(x, gain):
    return x * lax.rsqrt(jnp.mean(x * x, axis=-1, keepdims=True) + EPS) * gain


def _ffn_kernel(*refs, n_ff, emit_mix):
    if emit_mix:
        (x_ref, g_ref, wg_ref, wu_ref, wd_ref, g2_ref, ws_ref, bs_ref,
         o_ref, hn2_ref, small_ref, hn_ref, acc_ref) = refs
    else:
        x_ref, g_ref, wg_ref, wu_ref, wd_ref, o_ref, hn_ref, acc_ref = refs
    j = pl.program_id(1)

    @pl.when(j == 0)
    def _():
        hn_ref[...] = _rms(x_ref[...], g_ref[...]).astype(BF16)
        acc_ref[...] = jnp.zeros_like(acc_ref)

    hn = hn_ref[...]
    a = _dot(hn, wg_ref[...])
    b = _dot(hn, wu_ref[...])
    h = (a * jax.nn.sigmoid(a) * b).astype(BF16)
    acc_ref[...] += _dot(h, wd_ref[...])

    @pl.when(j == n_ff - 1)
    def _():
        y = x_ref[...] + 0.5 * acc_ref[...]
        o_ref[...] = y
        if emit_mix:
            hn2 = _rms(y, g2_ref[...]).astype(BF16)
            hn2_ref[...] = hn2
            small_ref[...] = _dot(hn2, ws_ref[...]) + bs_ref[...]


def _ffn(x, gain, wg, wu, wd, mix=None, *, tm=512, tf=512):
    n, d = x.shape
    f = wg.shape[1]
    assert n % tm == 0 and f % tf == 0
    n_ff = f // tf
    in_specs = [
        pl.BlockSpec((tm, d), lambda i, j: (i, 0)),
        pl.BlockSpec((1, d), lambda i, j: (0, 0)),
        pl.BlockSpec((d, tf), lambda i, j: (0, j)),
        pl.BlockSpec((d, tf), lambda i, j: (0, j)),
        pl.BlockSpec((tf, d), lambda i, j: (j, 0)),
    ]
    out_shape = [jax.ShapeDtypeStruct((n, d), F32)]
    out_specs = [pl.BlockSpec((tm, d), lambda i, j: (i, 0))]
    args = [x, gain, wg, wu, wd]
    if mix is not None:
        g2, ws, bs = mix
        in_specs += [
            pl.BlockSpec((1, d), lambda i, j: (0, 0)),
            pl.BlockSpec((d, _SMALL_WIDTH), lambda i, j: (0, 0)),
            pl.BlockSpec((1, _SMALL_WIDTH), lambda i, j: (0, 0)),
        ]
        out_shape += [jax.ShapeDtypeStruct((n, d), BF16), jax.ShapeDtypeStruct((n, _SMALL_WIDTH), F32)]
        out_specs += [pl.BlockSpec((tm, d), lambda i, j: (i, 0)),
                      pl.BlockSpec((tm, _SMALL_WIDTH), lambda i, j: (i, 0))]
        args += [g2, ws, bs]
    res = pl.pallas_call(
        functools.partial(_ffn_kernel, n_ff=n_ff, emit_mix=mix is not None),
        grid=(n // tm, n_ff),
        in_specs=in_specs,
        out_specs=out_specs,
        out_shape=out_shape,
        scratch_shapes=[pltpu.VMEM((tm, d), BF16), pltpu.VMEM((tm, d), F32)],
        compiler_params=_params("parallel", "arbitrary"),
        name="ffn_mix" if mix is not None else "ffn",
    )(*args)
    return res if mix is not None else res[0]


def _proj_kernel(a_ref, w_ref, b_ref, o_ref):
    o_ref[...] = (_dot(a_ref[...], w_ref[...]) + b_ref[...]).astype(o_ref.dtype)


def _proj(a, w, b, *, tm=1024, tn=1536):
    n, d = a.shape
    p = w.shape[1]
    assert n % tm == 0 and p % tn == 0
    return pl.pallas_call(
        _proj_kernel,
        grid=(p // tn, n // tm),
        in_specs=[
            pl.BlockSpec((tm, d), lambda j, i: (i, 0)),
            pl.BlockSpec((d, tn), lambda j, i: (0, j)),
            pl.BlockSpec((1, tn), lambda j, i: (0, j)),
        ],
        out_specs=pl.BlockSpec((tm, tn), lambda j, i: (i, j)),
        out_shape=jax.ShapeDtypeStruct((n, p), BF16),
        compiler_params=_params("parallel", "parallel"),
        name="proj",
    )(a, w, b)


def _prep_kernel(kv_ref, wcat_ref, pk_ref, w1k_ref, w2k_ref, pv_ref, w1v_ref, w2vt_ref,
                 gc_ref, gs_ref, gw_ref, kc_ref, vct_ref, ksa_ref, vst_ref, kwn_ref, vwt_ref, cmp_ref):
    dh = NSA_HEAD_DIM
    hid = w1k_ref.shape[1]
    rows = kv_ref.shape[0] // CMP_STRIDE

    cmp_ref[...] = kv_ref[:, 0:2 * dh].astype(F32)
    part = None
    for l in range(CMP_STRIDE):
        x_l = cmp_ref[pl.ds(l, rows, stride=CMP_STRIDE), :].astype(BF16)
        term = _dot(x_l, wcat_ref[l])
        part = term if part is None else part + term

    def hidden(first, second, pos_ref, w1_ref):
        pos = jnp.broadcast_to(pos_ref[...], (8, pos_ref.shape[1])).astype(BF16)
        pb = _dot(pos, w1_ref[...])[0:1, :]
        return jax.nn.gelu(first + pltpu.roll(second, rows - 1, axis=0) + pb).astype(BF16)

    hk = hidden(part[:, 0:hid], part[:, hid:2 * hid], pk_ref, w1k_ref)
    hv = hidden(part[:, 2 * hid:3 * hid], part[:, 3 * hid:4 * hid], pv_ref, w1v_ref)
    kc_ref[0] = _rms(_dot(hk, w2k_ref[...]), gc_ref[...]).astype(BF16)
    vct_ref[0] = lax.dot_general(w2vt_ref[...], hv, _NT, preferred_element_type=F32).astype(BF16)

    s = kv_ref.shape[0]
    slc = kv_ref[:, 2 * dh:4 * dh].astype(F32)
    ksn = _rms(slc[:, 0:dh], gs_ref[...]).astype(BF16)
    key_blk = lax.broadcasted_iota(jnp.int32, (s, dh), 0) // SLC_BLOCK
    blk = lax.broadcasted_iota(jnp.int32, (s, dh), 1)
    onehot = (key_blk == blk).astype(BF16)
    ksa_ref[0] = jnp.concatenate([ksn, onehot], axis=1)
    vst_ref[0] = slc.T[dh:2 * dh, :].astype(BF16)
    win = kv_ref[:, 4 * dh:6 * dh].astype(F32)
    kwn_ref[0] = _rms(win[:, 0:dh], gw_ref[...]).astype(BF16)
    vwt_ref[0] = win.T[dh:2 * dh, :].astype(BF16)


def _prep(proj, wcat, pos_k, w1k, w2k, pos_v, w1v, w2vt, gc, gs, gw, *, batch, groups):
    bg = batch * groups
    s = proj.shape[0] // batch
    rows = s // CMP_STRIDE
    dh = NSA_HEAD_DIM
    hid = w1k.shape[1]
    width = CMP_STRIDE * dh
    kv_w = 6 * dh
    kv_col = _P_KV // kv_w
    assert kv_col * kv_w == _P_KV and s % CMP_STRIDE == 0
    full = lambda shape: pl.BlockSpec(shape, lambda i: (0,) * len(shape))
    per = lambda shape: pl.BlockSpec((1,) + shape, lambda i: (i, 0, 0))
    return pl.pallas_call(
        _prep_kernel,
        grid=(bg,),
        in_specs=[pl.BlockSpec((s, kv_w), lambda i: (i // groups, kv_col + i % groups)),
                  full(wcat.shape),
                  full((1, 2 * width)), full((2 * width, hid)), full((hid, dh)),
                  full((1, 2 * width)), full((2 * width, hid)), full((dh, hid)),
                  full((1, dh)), full((1, dh)), full((1, dh))],
        out_specs=[per((rows, dh)), per((dh, rows)), per((s, 2 * dh)), per((dh, s)), per((s, dh)), per((dh, s))],
        out_shape=[jax.ShapeDtypeStruct((bg, rows, dh), BF16), jax.ShapeDtypeStruct((bg, dh, rows), BF16),
                   jax.ShapeDtypeStruct((bg, s, 2 * dh), BF16), jax.ShapeDtypeStruct((bg, dh, s), BF16),
                   jax.ShapeDtypeStruct((bg, s, dh), BF16), jax.ShapeDtypeStruct((bg, dh, s), BF16)],
        scratch_shapes=[pltpu.VMEM((s, 2 * dh), F32)],
        compiler_params=_params("parallel"),
        name="nsa_prep",
    )(proj, wcat, pos_k, w1k, w2k, pos_v, w1v, w2vt, gc, gs, gw)


def _attn_kernel(q_ref, small_ref, kc_ref, vct_ref, ksa_ref, vst_ref, kw_ref, vwt_ref, qg_ref, ovt_ref,
                 o_ref, qa_ref, m_ref, l_ref, acc_ref, ocmp_ref, score_ref, rank_ref, gate_ref,
                 *, tq, pair, groups):
    r = NSA_GROUP
    dh = NSA_HEAD_DIM
    n_cmp = kc_ref.shape[1]
    i = pl.program_id(1)
    q0 = i * tq
    head_lanes = [slice(h * tq, (h + 1) * tq) for h in range(r)]

    q_t = q_ref[...].astype(F32).T
    for h, lanes in enumerate(head_lanes):
        q = q_t[h * dh:(h + 1) * dh, :]
        inv = lax.rsqrt(jnp.mean(q * q, axis=0, keepdims=True) + EPS)
        qa_ref[0:dh, lanes] = (q * inv * qg_ref[...] * (dh ** -0.5 * LOG2_E)).astype(BF16)
    gate_ref[...] = small_ref[...].T
    slot = pl.multiple_of((pl.program_id(0) % groups) * _GATE_SLOT, _GATE_SLOT)
    gt = jax.nn.sigmoid(gate_ref[pl.ds(slot, _GATE_SLOT), :])

    tq_row = q0 + lax.broadcasted_iota(jnp.int32, (1, tq), 1)

    c_end = lax.broadcasted_iota(jnp.int32, (n_cmp, 1), 0) * CMP_STRIDE + (CMP_BLOCK - 1)
    cmask = c_end <= tq_row
    has_cmp = tq_row >= CMP_BLOCK - 1
    scs = [jnp.where(cmask, _dot(kc_ref[0], qa_ref[0:dh, lanes]), NEG_INF) for lanes in head_lanes]
    es = [jnp.exp2(sc - jnp.max(sc, axis=0, keepdims=True)) for sc in scs]
    p_cmps = [e * jnp.where(has_cmp, 1.0 / jnp.sum(e, axis=0, keepdims=True), 0.0) for e in es]
    o_cmps = [_dot(vct_ref[0], p_cmp.astype(BF16)) for p_cmp in p_cmps]
    for lanes, o_cmp in zip(head_lanes, o_cmps):
        ocmp_ref[:, lanes] = o_cmp
    p_sum = functools.reduce(jnp.add, p_cmps)
    ovt = ovt_ref[...]
    imp_t = sum(_dot(ovt, part) for part in _split3(p_sum))
    n_blk = imp_t.shape[0]
    blk = lax.broadcasted_iota(jnp.int32, (n_blk, 1), 0)
    cur = tq_row // SLC_BLOCK
    forced = (blk == 0) | (blk == cur) | (blk == cur - 1)
    score_ref[...] = jnp.where(forced, FORCE_SCORE, jnp.where(blk * SLC_BLOCK <= tq_row, imp_t, -1.0))
    rank_ref[...] = jnp.zeros_like(rank_ref)
    sub = 8
    last_blk = (q0 + tq - 1) // SLC_BLOCK
    for kg in range(0, n_blk, sub):
        @pl.when(kg <= last_blk)
        def _(kg=kg):
            for ng in range(0, n_blk, sub):
                sn = score_ref[ng:ng + sub, :]
                rn = rank_ref[ng:ng + sub, :]
                for k in range(kg, kg + sub):
                    sk = score_ref[k:k + 1, :]
                    if ng > kg:
                        ahead = sk >= sn
                    elif ng < kg:
                        ahead = sk > sn
                    else:
                        later = lax.broadcasted_iota(jnp.int32, (sub, 1), 0) > (k - kg)
                        ahead = (sk > sn) | (later & (sk == sn))
                    rn = jnp.where(ahead, rn + 1.0, rn)
                rank_ref[ng:ng + sub, :] = rn
    sel_bias = jnp.where(rank_ref[...] < SLC_TOPN, 0.0, NEG_INF).astype(BF16)
    for lanes in head_lanes:
        qa_ref[dh:2 * dh, lanes] = sel_bias

    def key_block(k_ref, vt_ref, kdim, k0, n_keys, window, init):
        start = pl.multiple_of(k0, tq)
        k_blk = k_ref[0, pl.ds(start, n_keys), :]
        vt = vt_ref[0, :, pl.ds(start, n_keys)]
        ss = [_dot(k_blk, qa_ref[0:kdim, lanes]) for lanes in head_lanes]
        if window is not None:
            back = tq_row - (start + lax.broadcasted_iota(jnp.int32, (n_keys, 1), 0))
            mask = back >= 0
            if window != float("inf"):
                mask = mask & (back < window)
            ss = [jnp.where(mask, s, NEG_INF) for s in ss]
        mxs = [jnp.max(s, axis=0, keepdims=True) for s in ss]
        if init:
            ps = [jnp.exp2(s - mx) for s, mx in zip(ss, mxs)]
            for lanes, mx, p in zip(head_lanes, mxs, ps):
                m_ref[:, lanes] = mx
                l_ref[:, lanes] = jnp.sum(p, axis=0, keepdims=True)
                acc_ref[:, lanes] = _dot(vt, p.astype(BF16))
        else:
            m_prevs = [m_ref[:, lanes] for lanes in head_lanes]
            m_news = [jnp.maximum(mp, mx) for mp, mx in zip(m_prevs, mxs)]
            alphas = [jnp.exp2(mp - mn) for mp, mn in zip(m_prevs, m_news)]
            ps = [jnp.exp2(s - mn) for s, mn in zip(ss, m_news)]
            pvs = [_dot(vt, p.astype(BF16)) for p in ps]
            for lanes, mn, al, p, pv in zip(head_lanes, m_news, alphas, ps, pvs):
                m_ref[:, lanes] = mn
                l_ref[:, lanes] = al * l_ref[:, lanes] + jnp.sum(p, axis=0, keepdims=True)
                acc_ref[:, lanes] = al * acc_ref[:, lanes] + pv

    key_block(ksa_ref, vst_ref, 2 * dh, q0, tq, float("inf"), True)

    def slc_body(kt, carry):
        key_block(ksa_ref, vst_ref, 2 * dh, kt * (pair * tq), pair * tq, None, False)
        return carry

    lax.fori_loop(0, i // pair, slc_body, 0)
    for rem in range(1, pair):
        @pl.when(i % pair >= rem)
        def _(rem=rem):
            key_block(ksa_ref, vst_ref, 2 * dh, (i - rem) * tq, tq, None, False)
    for h, lanes in enumerate(head_lanes):
        ocmp_ref[:, lanes] = (gt[3 * h:3 * h + 1, :] * ocmp_ref[:, lanes]
                              + gt[3 * h + 1:3 * h + 2, :] * (acc_ref[:, lanes] / l_ref[:, lanes]))

    key_block(kw_ref, vwt_ref, dh, jnp.maximum(q0 - WINDOW, 0), WINDOW + tq, WINDOW, True)
    outs = [ocmp_ref[:, lanes] + gt[3 * h + 2:3 * h + 3, :] * (acc_ref[:, lanes] / l_ref[:, lanes])
            for h, lanes in enumerate(head_lanes)]
    o_ref[...] = jnp.concatenate(outs, axis=0).T.astype(o_ref.dtype)


def _attn(proj, small, kc, vct, ksa, vst, kwn, vwt, q_gain_col, ovt, *, batch, groups, tq):
    n = proj.shape[0]
    s = n // batch
    nq = s // tq
    r, dh = NSA_GROUP, NSA_HEAD_DIM
    m_cols = r * tq
    q_col = _P_Q // (r * dh)
    assert q_col * r * dh == _P_Q and s % tq == 0
    assert WINDOW % tq == 0 and s >= WINDOW + tq and s // SLC_BLOCK <= dh and tq % SLC_BLOCK == 0
    assert groups * _GATE_SLOT <= _SMALL_WIDTH and 3 * r <= _GATE_SLOT
    n_cmp = kc.shape[1]
    per_bg = lambda shape: pl.BlockSpec((1,) + shape, lambda bg, i: (bg, 0, 0))
    return pl.pallas_call(
        functools.partial(_attn_kernel, tq=tq, pair=_ATTN_PAIR, groups=groups),
        grid=(batch * groups, nq),
        in_specs=[
            pl.BlockSpec((tq, r * dh), lambda bg, i: ((bg // groups) * nq + i, q_col + bg % groups)),
            pl.BlockSpec((tq, _SMALL_WIDTH), lambda bg, i: ((bg // groups) * nq + i, 0)),
            per_bg((n_cmp, dh)), per_bg((dh, n_cmp)),
            per_bg((s, 2 * dh)), per_bg((dh, s)), per_bg((s, dh)), per_bg((dh, s)),
            pl.BlockSpec((dh, 1), lambda bg, i: (0, 0)),
            pl.BlockSpec(ovt.shape, lambda bg, i: (0, 0)),
        ],
        out_specs=pl.BlockSpec((tq, r * dh), lambda bg, i: ((bg // groups) * nq + i, bg % groups)),
        out_shape=jax.ShapeDtypeStruct((n, groups * r * dh), BF16),
        scratch_shapes=[pltpu.VMEM((2 * dh, m_cols), BF16),
                        pltpu.VMEM((1, m_cols), F32), pltpu.VMEM((1, m_cols), F32),
                        pltpu.VMEM((dh, m_cols), F32), pltpu.VMEM((dh, m_cols), F32),
                        pltpu.VMEM((dh, tq), F32), pltpu.VMEM((dh, tq), F32),
                        pltpu.VMEM((_SMALL_WIDTH, tq), F32)],
        compiler_params=_params("parallel", "parallel"),
        name="nsa_attn",
    )(proj, small, kc, vct, ksa, vst, kwn, vwt, q_gain_col, ovt)


def _mlstm_kernel(mq_ref, mk_ref, mv_ref, mo_ref, small_ref, gt_ref, cw_ref, cb_ref, og_ref, cum_ref,
                  o_ref, xbuf_ref, c_ref, n_ref, m_ref, *, tt):
    hd = M_HEAD_DIM
    L = M_CHUNK
    width = M_WIDTH
    n_chunk = tt // L
    j = pl.program_id(1)

    @pl.when(j == 0)
    def _():
        xbuf_ref[0:8, :] = jnp.zeros((8, 2 * width), F32)
        c_ref[...] = jnp.zeros_like(c_ref)
        n_ref[...] = jnp.zeros_like(n_ref)
        m_ref[...] = jnp.zeros_like(m_ref)

    @pl.when(j > 0)
    def _():
        xbuf_ref[0:8, :] = xbuf_ref[tt:tt + 8, :]

    xbuf_ref[8:8 + tt, 0:width] = mq_ref[...].astype(F32)
    xbuf_ref[8:8 + tt, width:2 * width] = mk_ref[...].astype(F32)
    conv = cb_ref[...]
    for tap in range(CONV_WIDTH):
        off = 8 - (CONV_WIDTH - 1) + tap
        conv = conv + cw_ref[tap:tap + 1, :] * xbuf_ref[off:off + tt, :]
    qk = conv * jax.nn.sigmoid(conv)
    q_all = qk[:, 0:width]
    k_all = qk[:, width:2 * width] * (hd ** -0.5)

    small = small_ref[...]
    logf_col = jax.nn.log_sigmoid(small)
    gt = gt_ref[0].reshape(n_chunk * 8, L)
    logf_row = jax.nn.log_sigmoid(gt)
    cum_u = cum_ref[0]
    cum_l = cum_ref[1]
    a_row_all = sum(_dot(part, cum_u) for part in _split3(logf_row))
    causal = lax.broadcasted_iota(jnp.int32, (L, L), 1) <= lax.broadcasted_iota(jnp.int32, (L, L), 0)

    for c in range(n_chunk):
        rows = slice(c * L, (c + 1) * L)
        a_col_c = sum(_dot(cum_l, part) for part in _split3(logf_col[rows]))
        pending = []
        for h in range(M_HEADS):
            lanes = slice(h * hd, (h + 1) * hd)
            q_c = q_all[rows, lanes]
            k_c = k_all[rows, lanes]
            v_c = mv_ref[rows, lanes]
            q_b = q_c.astype(BF16)
            a_j = a_col_c[:, _S_MF + h:_S_MF + h + 1]
            li_j = small[rows, _S_MI + h:_S_MI + h + 1]
            a_s = a_row_all[c * 8 + M_HEADS + h:c * 8 + M_HEADS + h + 1, :]
            li_s = gt[c * 8 + h:c * 8 + h + 1, :]
            g_c = a_j[L - 1:L, :]
            m_st = m_ref[h:h + 1, 0:1]
            c_st = c_ref[h]
            n_st = n_ref[h]

            log_w = jnp.where(causal, a_j - a_s + li_s, NEG_INF)
            m_intra = jnp.max(log_w, axis=-1, keepdims=True)
            w = jnp.exp(log_w - m_intra) * lax.dot_general(q_b, k_c.astype(BF16), _NT, preferred_element_type=F32)
            num_intra = _dot(w.astype(BF16), v_c)
            den_intra = jnp.sum(w, axis=-1, keepdims=True)

            num_inter = _dot(q_b, c_st.astype(BF16))
            den_inter = jnp.sum(q_c * n_st, axis=-1, keepdims=True)

            log_inter = a_j + m_st
            m_comb = jnp.maximum(log_inter, m_intra)
            s_inter = jnp.exp(log_inter - m_comb)
            s_intra = jnp.exp(m_intra - m_comb)
            num = s_inter * num_inter + s_intra * num_intra
            den = s_inter * den_inter + s_intra * den_intra
            hcell = num / jnp.maximum(jnp.abs(den), jnp.exp(-m_comb))

            log_u = g_c - a_j + li_j
            m_new = jnp.maximum(g_c + m_st, jnp.max(log_u, axis=0, keepdims=True))
            decay = jnp.exp(g_c + m_st - m_new)
            uk = jnp.exp(log_u - m_new) * k_c
            c_new = decay * c_st + lax.dot_general(uk.astype(BF16), v_c, _TN, preferred_element_type=F32)
            n_new = decay * n_st + jnp.sum(uk, axis=0, keepdims=True)

            hn = _rms(hcell, og_ref[:, lanes])
            o_gate = jax.nn.sigmoid(mo_ref[rows, lanes].astype(F32))
            pending.append((h, lanes, c_new, n_new, m_new, (o_gate * hn).astype(o_ref.dtype)))
        for h, lanes, c_new, n_new, m_new, out in pending:
            c_ref[h] = c_new
            n_ref[h] = n_new
            m_ref[h:h + 1, :] = jnp.broadcast_to(m_new, (1, m_ref.shape[1]))
            o_ref[rows, lanes] = out


def _mlstm(proj, small, gate_t, conv_w, conv_b, out_gain, cum_u, *, batch, tt=256):
    n = proj.shape[0]
    s = n // batch
    assert s % tt == 0 and tt % M_CHUNK == 0
    nt = s // tt
    cq = _P_MQ // M_WIDTH
    assert cq * M_WIDTH == _P_MQ
    col = lambda k: pl.BlockSpec((tt, M_WIDTH), lambda b, j: (b * nt + j, k))
    full = lambda shape: pl.BlockSpec(shape, lambda b, j: (0,) * len(shape))
    return pl.pallas_call(
        functools.partial(_mlstm_kernel, tt=tt),
        grid=(batch, nt),
        in_specs=[col(cq), col(cq + 1), col(cq + 2), col(cq + 3),
                  pl.BlockSpec((tt, _SMALL_WIDTH), lambda b, j: (b * nt + j, 0)),
                  pl.BlockSpec((1, tt // M_CHUNK, 8, M_CHUNK), lambda b, j: (b, j, 0, 0)),
                  full(conv_w.shape), full(conv_b.shape), full(out_gain.shape), full(cum_u.shape)],
        out_specs=pl.BlockSpec((tt, M_WIDTH), lambda b, j: (b * nt + j, 0)),
        out_shape=jax.ShapeDtypeStruct((n, M_WIDTH), BF16),
        scratch_shapes=[pltpu.VMEM((tt + 8, 2 * M_WIDTH), F32),
                        pltpu.VMEM((M_HEADS, M_HEAD_DIM, M_HEAD_DIM), F32),
                        pltpu.VMEM((M_HEADS, 1, M_HEAD_DIM), F32),
                        pltpu.VMEM((8, 128), F32)],
        compiler_params=_params("parallel", "arbitrary"),
        name="mlstm",
    )(proj, proj, proj, proj, small, gate_t, conv_w, conv_b, out_gain, cum_u)


def _merge_kernel(x_ref, oa_ref, ob_ref, ga_ref, gb_ref, wa_ref, wb_ref, wo_ref, o_ref):
    ya = _dot(oa_ref[...], wa_ref[...])
    yb = _dot(ob_ref[...], wb_ref[...])
    merged = (jax.nn.sigmoid(ga_ref[...].astype(F32)) * ya
              + jax.nn.sigmoid(gb_ref[...].astype(F32)) * yb)
    o_ref[...] = x_ref[...] + _dot(merged.astype(BF16), wo_ref[...])


def _merge(x, o_nsa, h_m, proj, wa, wb, wo, *, tm=256):
    n, d = x.shape
    assert n % tm == 0 and _P_MERGE == 0
    const = lambda shape: pl.BlockSpec(shape, lambda i: (0, 0), pipeline_mode=pl.Buffered(1))
    return pl.pallas_call(
        _merge_kernel,
        grid=(n // tm,),
        in_specs=[
            pl.BlockSpec((tm, d), lambda i: (i, 0)),
            pl.BlockSpec((tm, o_nsa.shape[1]), lambda i: (i, 0)),
            pl.BlockSpec((tm, h_m.shape[1]), lambda i: (i, 0)),
            pl.BlockSpec((tm, d), lambda i: (i, 0)),
            pl.BlockSpec((tm, d), lambda i: (i, 1)),
            const(wa.shape), const(wb.shape), const(wo.shape),
        ],
        out_specs=pl.BlockSpec((tm, d), lambda i: (i, 0)),
        out_shape=jax.ShapeDtypeStruct((n, d), F32),
        compiler_params=_params("parallel"),
        name="merge",
    )(x, o_nsa, h_m, proj, proj, wa, wb, wo)


def _layer(x2, b, s, p):
    n = x2.shape[0]
    g, r, dh = NSA_KV_HEADS, NSA_GROUP, NSA_HEAD_DIM
    w_in, b_in = p["w_in"], p["b_in"]

    def cols(a, start, size):
        return a[..., start:start + size]

    def kv_group_major(a):
        kv = cols(a, _OFF_KV, 6 * NSA_KV_WIDTH)
        kv = kv.reshape(a.shape[:-1] + (6, g, dh))
        return jnp.swapaxes(kv, -3, -2).reshape(a.shape[:-1] + (6 * NSA_KV_WIDTH,))

    def gate_slots(a):
        gates = cols(a, _OFF_G, 3 * NSA_HEADS).reshape(a.shape[:-1] + (g, 3 * r))
        pad = [(0, 0)] * (gates.ndim - 1) + [(0, _GATE_SLOT - 3 * r)]
        return jnp.pad(gates, pad).reshape(a.shape[:-1] + (g * _GATE_SLOT,))

    def main_cols(a):
        return jnp.concatenate([cols(a, _OFF_MERGE, 2 * D_MODEL), cols(a, _OFF_Q, NSA_Q_WIDTH),
                                cols(a, _OFF_MQKV, 3 * M_WIDTH), cols(a, _OFF_MO, M_WIDTH),
                                kv_group_major(a)], axis=-1)

    def small_cols(a):
        used = jnp.concatenate([gate_slots(a), cols(a, _OFF_MI, M_HEADS), cols(a, _OFF_MF, M_HEADS)], axis=-1)
        pad = [(0, 0)] * (used.ndim - 1) + [(0, _SMALL_WIDTH - used.shape[-1])]
        return jnp.pad(used, pad)

    w_main = main_cols(w_in).astype(BF16)
    b_main = main_cols(b_in)[None, :]
    w_small = small_cols(w_in).astype(BF16)
    b_small = small_cols(b_in)[None, :]

    x1, hn, small = _ffn(x2, p["ffn1_norm"][None, :], p["ffn1_w_gate"].astype(BF16),
                         p["ffn1_w_up"].astype(BF16), p["ffn1_w_down"].astype(BF16),
                         mix=(p["mix_norm"][None, :], w_small, b_small))
    proj = _proj(hn, w_main, b_main)

    rows = s // CMP_STRIDE
    hid = p["cmp_w1_k"].shape[1]
    w1k4 = p["cmp_w1_k"].reshape(2, CMP_STRIDE, dh, hid)
    w1v4 = p["cmp_w1_v"].reshape(2, CMP_STRIDE, dh, hid)
    zeros = jnp.zeros((CMP_STRIDE, dh, 2 * hid), F32)
    wcat = jnp.concatenate([
        jnp.concatenate([w1k4[0], w1k4[1], zeros], axis=-1),
        jnp.concatenate([zeros, w1v4[0], w1v4[1]], axis=-1)], axis=1).astype(BF16)
    kc, vct, ksa, vst, kwn, vwt = _prep(
        proj, wcat,
        p["cmp_pos_k"].reshape(1, CMP_BLOCK * dh), p["cmp_w1_k"].astype(BF16), p["cmp_w2_k"].astype(BF16),
        p["cmp_pos_v"].reshape(1, CMP_BLOCK * dh), p["cmp_w1_v"].astype(BF16), p["cmp_w2_v"].T.astype(BF16),
        p["nsa_kc_gain"][None, :], p["nsa_ks_gain"][None, :], p["nsa_kw_gain"][None, :], batch=b, groups=g)
    c_start = np.arange(rows) * CMP_STRIDE
    b_start = np.arange(dh) * SLC_BLOCK
    ovt = ((c_start[None, :] < (b_start + SLC_BLOCK)[:, None]) & ((c_start + CMP_BLOCK)[None, :] > b_start[:, None])
           & (np.arange(rows) < rows - 1)[None, :])
    o_nsa = _attn(proj, small, kc, vct, ksa, vst, kwn, vwt, p["nsa_q_gain"][:, None],
                  jnp.asarray(ovt, BF16), batch=b, groups=g, tq=_ATTN_TQ)

    gate_t = small[:, _S_MI:_S_MI + 2 * M_HEADS].reshape(b, s // M_CHUNK, M_CHUNK, 2 * M_HEADS)
    gate_t = gate_t.transpose(0, 1, 3, 2)
    upper = np.triu(np.ones((M_CHUNK, M_CHUNK), np.float32))
    cum_u = jnp.asarray(np.stack([upper, upper.T]), BF16)
    h_m = _mlstm(proj, small, gate_t, p["m_conv_w"], p["m_conv_b"][None, :],
                 p["m_out_gain"].reshape(1, M_WIDTH), cum_u, batch=b)

    x3 = _merge(x1, o_nsa, h_m, proj, p["w_branch_nsa"].astype(BF16), p["w_branch_mlstm"].astype(BF16),
                p["w_out"].astype(BF16))
    return _ffn(x3, p["ffn2_norm"][None, :], p["ffn2_w_gate"].astype(BF16),
                p["ffn2_w_up"].astype(BF16), p["ffn2_w_down"].astype(BF16))


def kernel(x, ffn1_norm, ffn1_w_gate, ffn1_w_up, ffn1_w_down, mix_norm, w_in, b_in, nsa_q_gain, nsa_kc_gain, nsa_ks_gain, nsa_kw_gain, cmp_pos_k, cmp_w1_k, cmp_w2_k, cmp_pos_v, cmp_w1_v, cmp_w2_v, m_conv_w, m_conv_b, m_out_gain, w_branch_nsa, w_branch_mlstm, w_out, ffn2_norm, ffn2_w_gate, ffn2_w_up, ffn2_w_down):
    params = dict(ffn1_norm=ffn1_norm, ffn1_w_gate=ffn1_w_gate, ffn1_w_up=ffn1_w_up, ffn1_w_down=ffn1_w_down,
                  mix_norm=mix_norm, w_in=w_in, b_in=b_in, nsa_q_gain=nsa_q_gain, nsa_kc_gain=nsa_kc_gain,
                  nsa_ks_gain=nsa_ks_gain, nsa_kw_gain=nsa_kw_gain, cmp_pos_k=cmp_pos_k, cmp_w1_k=cmp_w1_k,
                  cmp_w2_k=cmp_w2_k, cmp_pos_v=cmp_pos_v, cmp_w1_v=cmp_w1_v, cmp_w2_v=cmp_w2_v,
                  m_conv_w=m_conv_w, m_conv_b=m_conv_b, m_out_gain=m_out_gain, w_branch_nsa=w_branch_nsa,
                  w_branch_mlstm=w_branch_mlstm, w_out=w_out, ffn2_norm=ffn2_norm, ffn2_w_gate=ffn2_w_gate,
                  ffn2_w_up=ffn2_w_up, ffn2_w_down=ffn2_w_down)
    b, s, d = x.shape
    h = x.reshape(b * s, d)
    for l in range(ffn1_norm.shape[0]):
        h = _layer(h, b, s, {k: v[l] for k, v in params.items()})
    return h.reshape(b, s, d)
```

```python
import functools

import numpy as np
import jax
import jax.numpy as jnp
from jax import lax
from jax.experimental import pallas as pl
from jax.experimental.pallas import tpu as pltpu

F32 = jnp.float32
BF16 = jnp.bfloat16

D_MODEL = 2048
D_FF = 5632
NSA_HEADS = 16
NSA_KV_HEADS = 4
NSA_GROUP = NSA_HEADS // NSA_KV_HEADS
NSA_HEAD_DIM = 64
NSA_Q_WIDTH = NSA_HEADS * NSA_HEAD_DIM
NSA_KV_WIDTH = NSA_KV_HEADS * NSA_HEAD_DIM
CMP_BLOCK = 32
CMP_STRIDE = 16
CMP_HIDDEN = 256
SLC_BLOCK = 64
SLC_TOPN = 16
WINDOW = 512
M_HEADS = 4
M_HEAD_DIM = 256
M_WIDTH = M_HEADS * M_HEAD_DIM
M_CHUNK = 64
CONV_WIDTH = 4
EPS = 1e-6
NEG_INF = -1e30
FORCE_SCORE = 1e4
LOG2_E = 1.4426950408889634

_OFF_Q = 0
_OFF_KV = _OFF_Q + NSA_Q_WIDTH
_OFF_G = _OFF_KV + 6 * NSA_KV_WIDTH
_OFF_MQKV = _OFF_G + 3 * NSA_HEADS
_OFF_MI = _OFF_MQKV + 3 * M_WIDTH
_OFF_MF = _OFF_MI + M_HEADS
_OFF_MO = _OFF_MF + M_HEADS
_OFF_MERGE = _OFF_MO + M_WIDTH
_IN_WIDTH = _OFF_MERGE + 2 * D_MODEL

_P_MERGE = 0
_P_Q = 2 * D_MODEL
_P_MQ = _P_Q + NSA_Q_WIDTH
_P_MO = _P_MQ + 3 * M_WIDTH
_P_KV = _P_MO + M_WIDTH
_P_WIDTH = _P_KV + 6 * NSA_KV_WIDTH
_SMALL_WIDTH = 128
_GATE_SLOT = 16
_S_MI = NSA_KV_HEADS * _GATE_SLOT
_S_MF = _S_MI + M_HEADS

VMEM_LIMIT_BYTES = 56 * 1024 * 1024
_ATTN_TQ = 256
_ATTN_PAIR = 2

_NT = (((1,), (1,)), ((), ()))
_TN = (((0,), (0,)), ((), ()))


def _params(*sem):
    return pltpu.CompilerParams(dimension_semantics=sem, vmem_limit_bytes=VMEM_LIMIT_BYTES)


def _dot(a, b):
    return jnp.dot(a, b, preferred_element_type=F32)


def _split3(x):
    hi = x.astype(BF16)
    r1 = x - hi.astype(F32)
    mid = r1.astype(BF16)
    lo = (r1 - mid.astype(F32)).astype(BF16)
    return hi, mid, lo


def _rms(x, gain):
    return x * lax.rsqrt(jnp.mean(x * x, axis=-1, keepdims=True) + EPS) * gain


def _ffn_kernel(*refs, n_ff, emit_mix):
    if emit_mix:
        (x_ref, g_ref, wg_ref, wu_ref, wd_ref, g2_ref, ws_ref, bs_ref,
         o_ref, hn2_ref, small_ref, hn_ref, acc_ref) = refs
    else:
        x_ref, g_ref, wg_ref, wu_ref, wd_ref, o_ref, hn_ref, acc_ref = refs
    j = pl.program_id(1)

    @pl.when(j == 0)
    def _():
        hn_ref[...] = _rms(x_ref[...], g_ref[...]).astype(BF16)
        acc_ref[...] = jnp.zeros_like(acc_ref)

    hn = hn_ref[...]
    a = _dot(hn, wg_ref[...])
    b = _dot(hn, wu_ref[...])
    h = (a * jax.nn.sigmoid(a) * b).astype(BF16)
    acc_ref[...] += _dot(h, wd_ref[...])

    @pl.when(j == n_ff - 1)
    def _():
        y = x_ref[...] + 0.5 * acc_ref[...]
        o_ref[...] = y
        if emit_mix:
            hn2 = _rms(y, g2_ref[...]).astype(BF16)
            hn2_ref[...] = hn2
            small_ref[...] = _dot(hn2, ws_ref[...]) + bs_ref[...]


def _ffn(x, gain, wg, wu, wd, mix=None, *, tm=512, tf=512):
    n, d = x.shape
    f = wg.shape[1]
    assert n % tm == 0 and f % tf == 0
    n_ff = f // tf
    in_specs = [
        pl.BlockSpec((tm, d), lambda i, j: (i, 0)),
        pl.BlockSpec((1, d), lambda i, j: (0, 0)),
        pl.BlockSpec((d, tf), lambda i, j: (0, j)),
        pl.BlockSpec((d, tf), lambda i, j: (0, j)),
        pl.BlockSpec((tf, d), lambda i, j: (j, 0)),
    ]
    out_shape = [jax.ShapeDtypeStruct((n, d), F32)]
    out_specs = [pl.BlockSpec((tm, d), lambda i, j: (i, 0))]
    args = [x, gain, wg, wu, wd]
    if mix is not None:
        g2, ws, bs = mix
        in_specs += [
            pl.BlockSpec((1, d), lambda i, j: (0, 0)),
            pl.BlockSpec((d, _SMALL_WIDTH), lambda i, j: (0, 0)),
            pl.BlockSpec((1, _SMALL_WIDTH), lambda i, j: (0, 0)),
        ]
        out_shape += [jax.ShapeDtypeStruct((n, d), BF16), jax.ShapeDtypeStruct((n, _SMALL_WIDTH), F32)]
        out_specs += [pl.BlockSpec((tm, d), lambda i, j: (i, 0)),
                      pl.BlockSpec((tm, _SMALL_WIDTH), lambda i, j: (i, 0))]
        args += [g2, ws, bs]
    res = pl.pallas_call(
        functools.partial(_ffn_kernel, n_ff=n_ff, emit_mix=mix is not None),
        grid=(n // tm, n_ff),
        in_specs=in_specs,
        out_specs=out_specs,
        out_shape=out_shape,
        scratch_shapes=[pltpu.VMEM((tm, d), BF16), pltpu.VMEM((tm, d), F32)],
        compiler_params=_params("parallel", "arbitrary"),
        name="ffn_mix" if mix is not None else "ffn",
    )(*args)
    return res if mix is not None else res[0]


def _proj_kernel(a_ref, w_ref, b_ref, o_ref):
    o_ref[...] = (_dot(a_ref[...], w_ref[...]) + b_ref[...]).astype(o_ref.dtype)


def _proj(a, w, b, *, tm=1024, tn=1536):
    n, d = a.shape
    p = w.shape[1]
    assert n % tm == 0 and p % tn == 0
    return pl.pallas_call(
        _proj_kernel,
        grid=(p // tn, n // tm),
        in_specs=[
            pl.BlockSpec((tm, d), lambda j, i: (i, 0)),
            pl.BlockSpec((d, tn), lambda j, i: (0, j)),
            pl.BlockSpec((1, tn), lambda j, i: (0, j)),
        ],
        out_specs=pl.BlockSpec((tm, tn), lambda j, i: (i, j)),
        out_shape=jax.ShapeDtypeStruct((n, p), BF16),
        compiler_params=_params("parallel", "parallel"),
        name="proj",
    )(a, w, b)


def _prep_kernel(kv_ref, wcat_ref, pk_ref, w1k_ref, w2k_ref, pv_ref, w1v_ref, w2vt_ref,
                 gc_ref, gs_ref, gw_ref, kc_ref, vct_ref, ksa_ref, vst_ref, kwn_ref, vwt_ref, cmp_ref):
    dh = NSA_HEAD_DIM
    hid = w1k_ref.shape[1]
    rows = kv_ref.shape[0] // CMP_STRIDE

    cmp_ref[...] = kv_ref[:, 0:2 * dh].astype(F32)
    part = None
    for l in range(CMP_STRIDE):
        x_l = cmp_ref[pl.ds(l, rows, stride=CMP_STRIDE), :].astype(BF16)
        term = _dot(x_l, wcat_ref[l])
        part = term if part is None else part + term

    def hidden(first, second, pos_ref, w1_ref):
        pos = jnp.broadcast_to(pos_ref[...], (8, pos_ref.shape[1])).astype(BF16)
        pb = _dot(pos, w1_ref[...])[0:1, :]
        return jax.nn.gelu(first + pltpu.roll(second, rows - 1, axis=0) + pb).astype(BF16)

    hk = hidden(part[:, 0:hid], part[:, hid:2 * hid], pk_ref, w1k_ref)
    hv = hidden(part[:, 2 * hid:3 * hid], part[:, 3 * hid:4 * hid], pv_ref, w1v_ref)
    kc_ref[0] = _rms(_dot(hk, w2k_ref[...]), gc_ref[...]).astype(BF16)
    vct_ref[0] = lax.dot_general(w2vt_ref[...], hv, _NT, preferred_element_type=F32).astype(BF16)

    s = kv_ref.shape[0]
    slc = kv_ref[:, 2 * dh:4 * dh].astype(F32)
    ksn = _rms(slc[:, 0:dh], gs_ref[...]).astype(BF16)
    key_blk = lax.broadcasted_iota(jnp.int32, (s, dh), 0) // SLC_BLOCK
    blk = lax.broadcasted_iota(jnp.int32, (s, dh), 1)
    onehot = (key_blk == blk).astype(BF16)
    ksa_ref[0] = jnp.concatenate([ksn, onehot], axis=1)
    vst_ref[0] = slc.T[dh:2 * dh, :].astype(BF16)
    win = kv_ref[:, 4 * dh:6 * dh].astype(F32)
    kwn_ref[0] = _rms(win[:, 0:dh], gw_ref[...]).astype(BF16)
    vwt_ref[0] = win.T[dh:2 * dh, :].astype(BF16)


def _prep(proj, wcat, pos_k, w1k, w2k, pos_v, w1v, w2vt, gc, gs, gw, *, batch, groups):
    bg = batch * groups
    s = proj.shape[0] // batch
    rows = s // CMP_STRIDE
    dh = NSA_HEAD_DIM
    hid = w1k.shape[1]
    width = CMP_STRIDE * dh
    kv_w = 6 * dh
    kv_col = _P_KV // kv_w
    assert kv_col * kv_w == _P_KV and s % CMP_STRIDE == 0
    full = lambda shape: pl.BlockSpec(shape, lambda i: (0,) * len(shape))
    per = lambda shape: pl.BlockSpec((1,) + shape, lambda i: (i, 0, 0))
    return pl.pallas_call(
        _prep_kernel,
        grid=(bg,),
        in_specs=[pl.BlockSpec((s, kv_w), lambda i: (i // groups, kv_col + i % groups)),
                  full(wcat.shape),
                  full((1, 2 * width)), full((2 * width, hid)), full((hid, dh)),
                  full((1, 2 * width)), full((2 * width, hid)), full((dh, hid)),
                  full((1, dh)), full((1, dh)), full((1, dh))],
        out_specs=[per((rows, dh)), per((dh, rows)), per((s, 2 * dh)), per((dh, s)), per((s, dh)), per((dh, s))],
        out_shape=[jax.ShapeDtypeStruct((bg, rows, dh), BF16), jax.ShapeDtypeStruct((bg, dh, rows), BF16),
                   jax.ShapeDtypeStruct((bg, s, 2 * dh), BF16), jax.ShapeDtypeStruct((bg, dh, s), BF16),
                   jax.ShapeDtypeStruct((bg, s, dh), BF16), jax.ShapeDtypeStruct((bg, dh, s), BF16)],
        scratch_shapes=[pltpu.VMEM((s, 2 * dh), F32)],
        compiler_params=_params("parallel"),
        name="nsa_prep",
    )(proj, wcat, pos_k, w1k, w2k, pos_v, w1v, w2vt, gc, gs, gw)


def _attn_kernel(q_ref, small_ref, kc_ref, vct_ref, ksa_ref, vst_ref, kw_ref, vwt_ref, qg_ref, ovt_ref,
                 o_ref, qa_ref, m_ref, l_ref, acc_ref, ocmp_ref, owin_ref, score_ref, rank_ref, gate_ref,
                 *, tq, pair, groups):
    r = NSA_GROUP
    dh = NSA_HEAD_DIM
    n_cmp = kc_ref.shape[1]
    i = pl.program_id(1)
    q0 = i * tq
    head_lanes = [slice(h * tq, (h + 1) * tq) for h in range(r)]

    q_t = q_ref[...].astype(F32).T
    for h, lanes in enumerate(head_lanes):
        q = q_t[h * dh:(h + 1) * dh, :]
        inv = lax.rsqrt(jnp.mean(q * q, axis=0, keepdims=True) + EPS)
        qa_ref[0:dh, lanes] = (q * inv * qg_ref[...] * (dh ** -0.5 * LOG2_E)).astype(BF16)
    gate_ref[...] = small_ref[...].T
    slot = pl.multiple_of((pl.program_id(0) % groups) * _GATE_SLOT, _GATE_SLOT)
    gt = jax.nn.sigmoid(gate_ref[pl.ds(slot, _GATE_SLOT), :])

    tq_row = q0 + lax.broadcasted_iota(jnp.int32, (1, tq), 1)

    w_start = pl.multiple_of(jnp.maximum(q0 - WINDOW, 0), tq)
    kw_blk = kw_ref[0, pl.ds(w_start, WINDOW + tq), :]
    vw_t = vwt_ref[0, :, pl.ds(w_start, WINDOW + tq)]
    w_back = tq_row - (w_start + lax.broadcasted_iota(jnp.int32, (WINDOW + tq, 1), 0))
    w_mask = (w_back >= 0) & (w_back < WINDOW)
    w_ss = [jnp.where(w_mask, _dot(kw_blk, qa_ref[0:dh, lanes]), NEG_INF) for lanes in head_lanes]
    w_ps = [jnp.exp2(s - jnp.max(s, axis=0, keepdims=True)) for s in w_ss]
    w_os = [_dot(vw_t, p.astype(BF16)) * (1.0 / jnp.sum(p, axis=0, keepdims=True)) for p in w_ps]
    for lanes, o_win in zip(head_lanes, w_os):
        owin_ref[:, lanes] = o_win

    c_end = lax.broadcasted_iota(jnp.int32, (n_cmp, 1), 0) * CMP_STRIDE + (CMP_BLOCK - 1)
    cmask = c_end <= tq_row
    has_cmp = tq_row >= CMP_BLOCK - 1
    scs = [jnp.where(cmask, _dot(kc_ref[0], qa_ref[0:dh, lanes]), NEG_INF) for lanes in head_lanes]
    es = [jnp.exp2(sc - jnp.max(sc, axis=0, keepdims=True)) for sc in scs]
    p_cmps = [e * jnp.where(has_cmp, 1.0 / jnp.sum(e, axis=0, keepdims=True), 0.0) for e in es]
    o_cmps = [_dot(vct_ref[0], p_cmp.astype(BF16)) for p_cmp in p_cmps]
    for lanes, o_cmp in zip(head_lanes, o_cmps):
        ocmp_ref[:, lanes] = o_cmp
    p_sum = functools.reduce(jnp.add, p_cmps)
    ovt = ovt_ref[...]
    imp_t = sum(_dot(ovt, part) for part in _split3(p_sum))
    n_blk = imp_t.shape[0]
    blk = lax.broadcasted_iota(jnp.int32, (n_blk, 1), 0)
    cur = tq_row // SLC_BLOCK
    forced = (blk == 0) | (blk == cur) | (blk == cur - 1)
    score_ref[...] = jnp.where(forced, FORCE_SCORE, jnp.where(blk * SLC_BLOCK <= tq_row, imp_t, -1.0))
    rank_ref[...] = jnp.zeros_like(rank_ref)
    sub = 8
    last_blk = (q0 + tq - 1) // SLC_BLOCK
    for kg in range(0, n_blk, sub):
        @pl.when(kg <= last_blk)
        def _(kg=kg):
            for ng in range(0, n_blk, sub):
                sn = score_ref[ng:ng + sub, :]
                rn = rank_ref[ng:ng + sub, :]
                for k in range(kg, kg + sub):
                    sk = score_ref[k:k + 1, :]
                    if ng > kg:
                        ahead = sk >= sn
                    elif ng < kg:
                        ahead = sk > sn
                    else:
                        later = lax.broadcasted_iota(jnp.int32, (sub, 1), 0) > (k - kg)
                        ahead = (sk > sn) | (later & (sk == sn))
                    rn = jnp.where(ahead, rn + 1.0, rn)
                rank_ref[ng:ng + sub, :] = rn
    sel_bias = jnp.where(rank_ref[...] < SLC_TOPN, 0.0, NEG_INF).astype(BF16)
    for lanes in head_lanes:
        qa_ref[dh:2 * dh, lanes] = sel_bias

    def key_block(k_ref, vt_ref, kdim, k0, n_keys, window, init):
        start = pl.multiple_of(k0, tq)
        k_blk = k_ref[0, pl.ds(start, n_keys), :]
        vt = vt_ref[0, :, pl.ds(start, n_keys)]
        ss = [_dot(k_blk, qa_ref[0:kdim, lanes]) for lanes in head_lanes]
        if window is not None:
            back = tq_row - (start + lax.broadcasted_iota(jnp.int32, (n_keys, 1), 0))
            mask = back >= 0
            if window != float("inf"):
                mask = mask & (back < window)
            ss = [jnp.where(mask, s, NEG_INF) for s in ss]
        mxs = [jnp.max(s, axis=0, keepdims=True) for s in ss]
        if init:
            ps = [jnp.exp2(s - mx) for s, mx in zip(ss, mxs)]
            for lanes, mx, p in zip(head_lanes, mxs, ps):
                m_ref[:, lanes] = mx
                l_ref[:, lanes] = jnp.sum(p, axis=0, keepdims=True)
                acc_ref[:, lanes] = _dot(vt, p.astype(BF16))
        else:
            m_prevs = [m_ref[:, lanes] for lanes in head_lanes]
            m_news = [jnp.maximum(mp, mx) for mp, mx in zip(m_prevs, mxs)]
            alphas = [jnp.exp2(mp - mn) for mp, mn in zip(m_prevs, m_news)]
            ps = [jnp.exp2(s - mn) for s, mn in zip(ss, m_news)]
            pvs = [_dot(vt, p.astype(BF16)) for p in ps]
            for lanes, mn, al, p, pv in zip(head_lanes, m_news, alphas, ps, pvs):
                m_ref[:, lanes] = mn
                l_ref[:, lanes] = al * l_ref[:, lanes] + jnp.sum(p, axis=0, keepdims=True)
                acc_ref[:, lanes] = al * acc_ref[:, lanes] + pv

    key_block(ksa_ref, vst_ref, 2 * dh, q0, tq, float("inf"), True)

    def slc_body(kt, carry):
        key_block(ksa_ref, vst_ref, 2 * dh, kt * (pair * tq), pair * tq, None, False)
        return carry

    lax.fori_loop(0, i // pair, slc_body, 0)
    for rem in range(1, pair):
        @pl.when(i % pair >= rem)
        def _(rem=rem):
            key_block(ksa_ref, vst_ref, 2 * dh, (i - rem) * tq, tq, None, False)
    for h, lanes in enumerate(head_lanes):
        ocmp_ref[:, lanes] = (gt[3 * h:3 * h + 1, :] * ocmp_ref[:, lanes]
                              + gt[3 * h + 1:3 * h + 2, :] * (acc_ref[:, lanes] / l_ref[:, lanes]))

    outs = [ocmp_ref[:, lanes] + gt[3 * h + 2:3 * h + 3, :] * owin_ref[:, lanes]
            for h, lanes in enumerate(head_lanes)]
    o_ref[...] = jnp.concatenate(outs, axis=0).T.astype(o_ref.dtype)


def _attn(proj, small, kc, vct, ksa, vst, kwn, vwt, q_gain_col, ovt, *, batch, groups, tq):
    n = proj.shape[0]
    s = n // batch
    nq = s // tq
    r, dh = NSA_GROUP, NSA_HEAD_DIM
    m_cols = r * tq
    q_col = _P_Q // (r * dh)
    assert q_col * r * dh == _P_Q and s % tq == 0
    assert WINDOW % tq == 0 and s >= WINDOW + tq and s // SLC_BLOCK <= dh and tq % SLC_BLOCK == 0
    assert groups * _GATE_SLOT <= _SMALL_WIDTH and 3 * r <= _GATE_SLOT
    n_cmp = kc.shape[1]
    per_bg = lambda shape: pl.BlockSpec((1,) + shape, lambda bg, i: (bg, 0, 0))
    return pl.pallas_call(
        functools.partial(_attn_kernel, tq=tq, pair=_ATTN_PAIR, groups=groups),
        grid=(batch * groups, nq),
        in_specs=[
            pl.BlockSpec((tq, r * dh), lambda bg, i: ((bg // groups) * nq + i, q_col + bg % groups)),
            pl.BlockSpec((tq, _SMALL_WIDTH), lambda bg, i: ((bg // groups) * nq + i, 0)),
            per_bg((n_cmp, dh)), per_bg((dh, n_cmp)),
            per_bg((s, 2 * dh)), per_bg((dh, s)), per_bg((s, dh)), per_bg((dh, s)),
            pl.BlockSpec((dh, 1), lambda bg, i: (0, 0)),
            pl.BlockSpec(ovt.shape, lambda bg, i: (0, 0)),
        ],
        out_specs=pl.BlockSpec((tq, r * dh), lambda bg, i: ((bg // groups) * nq + i, bg % groups)),
        out_shape=jax.ShapeDtypeStruct((n, groups * r * dh), BF16),
        scratch_shapes=[pltpu.VMEM((2 * dh, m_cols), BF16),
                        pltpu.VMEM((1, m_cols), F32), pltpu.VMEM((1, m_cols), F32),
                        pltpu.VMEM((dh, m_cols), F32), pltpu.VMEM((dh, m_cols), F32),
                        pltpu.VMEM((dh, m_cols), F32),
                        pltpu.VMEM((dh, tq), F32), pltpu.VMEM((dh, tq), F32),
                        pltpu.VMEM((_SMALL_WIDTH, tq), F32)],
        compiler_params=_params("parallel", "parallel"),
        name="nsa_attn",
    )(proj, small, kc, vct, ksa, vst, kwn, vwt, q_gain_col, ovt)


def _mlstm_kernel(mq_ref, mk_ref, mv_ref, mo_ref, small_ref, gt_ref, cw_ref, cb_ref, og_ref, cum_ref,
                  o_ref, xbuf_ref, c_ref, n_ref, m_ref, *, tt):
    hd = M_HEAD_DIM
    L = M_CHUNK
    width = M_WIDTH
    n_chunk = tt // L
    j = pl.program_id(1)

    @pl.when(j == 0)
    def _():
        xbuf_ref[0:8, :] = jnp.zeros((8, 2 * width), F32)
        c_ref[...] = jnp.zeros_like(c_ref)
        n_ref[...] = jnp.zeros_like(n_ref)
        m_ref[...] = jnp.zeros_like(m_ref)

    @pl.when(j > 0)
    def _():
        xbuf_ref[0:8, :] = xbuf_ref[tt:tt + 8, :]

    xbuf_ref[8:8 + tt, 0:width] = mq_ref[...].astype(F32)
    xbuf_ref[8:8 + tt, width:2 * width] = mk_ref[...].astype(F32)
    conv = cb_ref[...]
    for tap in range(CONV_WIDTH):
        off = 8 - (CONV_WIDTH - 1) + tap
        conv = conv + cw_ref[tap:tap + 1, :] * xbuf_ref[off:off + tt, :]
    qk = conv * jax.nn.sigmoid(conv)
    q_all = qk[:, 0:width]
    k_all = qk[:, width:2 * width] * (hd ** -0.5)

    small = small_ref[...]
    logf_col = jax.nn.log_sigmoid(small)
    gt = gt_ref[0].reshape(n_chunk * 8, L)
    logf_row = jax.nn.log_sigmoid(gt)
    cum_u = cum_ref[0]
    cum_l = cum_ref[1]
    a_row_all = sum(_dot(part, cum_u) for part in _split3(logf_row))
    causal = lax.broadcasted_iota(jnp.int32, (L, L), 1) <= lax.broadcasted_iota(jnp.int32, (L, L), 0)

    for c in range(n_chunk):
        rows = slice(c * L, (c + 1) * L)
        a_col_c = sum(_dot(cum_l, part) for part in _split3(logf_col[rows]))
        pending = []
        for h in range(M_HEADS):
            lanes = slice(h * hd, (h + 1) * hd)
            q_c = q_all[rows, lanes]
            k_c = k_all[rows, lanes]
            v_c = mv_ref[rows, lanes]
            q_b = q_c.astype(BF16)
            a_j = a_col_c[:, _S_MF + h:_S_MF + h + 1]
            li_j = small[rows, _S_MI + h:_S_MI + h + 1]
            a_s = a_row_all[c * 8 + M_HEADS + h:c * 8 + M_HEADS + h + 1, :]
            li_s = gt[c * 8 + h:c * 8 + h + 1, :]
            g_c = a_j[L - 1:L, :]
            m_st = m_ref[h:h + 1, 0:1]
            c_st = c_ref[h]
            n_st = n_ref[h]

            log_w = jnp.where(causal, a_j - a_s + li_s, NEG_INF)
            m_intra = jnp.max(log_w, axis=-1, keepdims=True)
            w = jnp.exp(log_w - m_intra) * lax.dot_general(q_b, k_c.astype(BF16), _NT, preferred_element_type=F32)
            num_intra = _dot(w.astype(BF16), v_c)
            den_intra = jnp.sum(w, axis=-1, keepdims=True)

            num_inter = _dot(q_b, c_st.astype(BF16))
            den_inter = jnp.sum(q_c * n_st, axis=-1, keepdims=True)

            log_inter = a_j + m_st
            m_comb = jnp.maximum(log_inter, m_intra)
            s_inter = jnp.exp(log_inter - m_comb)
            s_intra = jnp.exp(m_intra - m_comb)
            num = s_inter * num_inter + s_intra * num_intra
            den = s_inter * den_inter + s_intra * den_intra
            hcell = num / jnp.maximum(jnp.abs(den), jnp.exp(-m_comb))

            log_u = g_c - a_j + li_j
            m_new = jnp.maximum(g_c + m_st, jnp.max(log_u, axis=0, keepdims=True))
            decay = jnp.exp(g_c + m_st - m_new)
            uk = jnp.exp(log_u - m_new) * k_c
            c_new = decay * c_st + lax.dot_general(uk.astype(BF16), v_c, _TN, preferred_element_type=F32)
            n_new = decay * n_st + jnp.sum(uk, axis=0, keepdims=True)

            hn = _rms(hcell, og_ref[:, lanes])
            o_gate = jax.nn.sigmoid(mo_ref[rows, lanes].astype(F32))
            pending.append((h, lanes, c_new, n_new, m_new, (o_gate * hn).astype(o_ref.dtype)))
        for h, lanes, c_new, n_new, m_new, out in pending:
            c_ref[h] = c_new
            n_ref[h] = n_new
            m_ref[h:h + 1, :] = jnp.broadcast_to(m_new, (1, m_ref.shape[1]))
            o_ref[rows, lanes] = out


def _mlstm(proj, small, gate_t, conv_w, conv_b, out_gain, cum_u, *, batch, tt=256):
    n = proj.shape[0]
    s = n // batch
    assert s % tt == 0 and tt % M_CHUNK == 0
    nt = s // tt
    cq = _P_MQ // M_WIDTH
    assert cq * M_WIDTH == _P_MQ
    col = lambda k: pl.BlockSpec((tt, M_WIDTH), lambda b, j: (b * nt + j, k))
    full = lambda shape: pl.BlockSpec(shape, lambda b, j: (0,) * len(shape))
    return pl.pallas_call(
        functools.partial(_mlstm_kernel, tt=tt),
        grid=(batch, nt),
        in_specs=[col(cq), col(cq + 1), col(cq + 2), col(cq + 3),
                  pl.BlockSpec((tt, _SMALL_WIDTH), lambda b, j: (b * nt + j, 0)),
                  pl.BlockSpec((1, tt // M_CHUNK, 8, M_CHUNK), lambda b, j: (b, j, 0, 0)),
                  full(conv_w.shape), full(conv_b.shape), full(out_gain.shape), full(cum_u.shape)],
        out_specs=pl.BlockSpec((tt, M_WIDTH), lambda b, j: (b * nt + j, 0)),
        out_shape=jax.ShapeDtypeStruct((n, M_WIDTH), BF16),
        scratch_shapes=[pltpu.VMEM((tt + 8, 2 * M_WIDTH), F32),
                        pltpu.VMEM((M_HEADS, M_HEAD_DIM, M_HEAD_DIM), F32),
                        pltpu.VMEM((M_HEADS, 1, M_HEAD_DIM), F32),
                        pltpu.VMEM((8, 128), F32)],
        compiler_params=_params("parallel", "arbitrary"),
        name="mlstm",
    )(proj, proj, proj, proj, small, gate_t, conv_w, conv_b, out_gain, cum_u)


def _merge_kernel(x_ref, oa_ref, ob_ref, ga_ref, gb_ref, wa_ref, wb_ref, wo_ref, o_ref):
    ya = _dot(oa_ref[...], wa_ref[...])
    yb = _dot(ob_ref[...], wb_ref[...])
    merged = (jax.nn.sigmoid(ga_ref[...].astype(F32)) * ya
              + jax.nn.sigmoid(gb_ref[...].astype(F32)) * yb)
    o_ref[...] = x_ref[...] + _dot(merged.astype(BF16), wo_ref[...])


def _merge(x, o_nsa, h_m, proj, wa, wb, wo, *, tm=256):
    n, d = x.shape
    assert n % tm == 0 and _P_MERGE == 0
    const = lambda shape: pl.BlockSpec(shape, lambda i: (0, 0), pipeline_mode=pl.Buffered(1))
    return pl.pallas_call(
        _merge_kernel,
        grid=(n // tm,),
        in_specs=[
            pl.BlockSpec((tm, d), lambda i: (i, 0)),
            pl.BlockSpec((tm, o_nsa.shape[1]), lambda i: (i, 0)),
            pl.BlockSpec((tm, h_m.shape[1]), lambda i: (i, 0)),
            pl.BlockSpec((tm, d), lambda i: (i, 0)),
            pl.BlockSpec((tm, d), lambda i: (i, 1)),
            const(wa.shape), const(wb.shape), const(wo.shape),
        ],
        out_specs=pl.BlockSpec((tm, d), lambda i: (i, 0)),
        out_shape=jax.ShapeDtypeStruct((n, d), F32),
        compiler_params=_params("parallel"),
        name="merge",
    )(x, o_nsa, h_m, proj, proj, wa, wb, wo)


def _layer(x2, b, s, p):
    n = x2.shape[0]
    g, r, dh = NSA_KV_HEADS, NSA_GROUP, NSA_HEAD_DIM
    w_in, b_in = p["w_in"], p["b_in"]

    def cols(a, start, size):
        return a[..., start:start + size]

    def kv_group_major(a):
        kv = cols(a, _OFF_KV, 6 * NSA_KV_WIDTH)
        kv = kv.reshape(a.shape[:-1] + (6, g, dh))
        return jnp.swapaxes(kv, -3, -2).reshape(a.shape[:-1] + (6 * NSA_KV_WIDTH,))

    def gate_slots(a):
        gates = cols(a, _OFF_G, 3 * NSA_HEADS).reshape(a.shape[:-1] + (g, 3 * r))
        pad = [(0, 0)] * (gates.ndim - 1) + [(0, _GATE_SLOT - 3 * r)]
        return jnp.pad(gates, pad).reshape(a.shape[:-1] + (g * _GATE_SLOT,))

    def main_cols(a):
        return jnp.concatenate([cols(a, _OFF_MERGE, 2 * D_MODEL), cols(a, _OFF_Q, NSA_Q_WIDTH),
                                cols(a, _OFF_MQKV, 3 * M_WIDTH), cols(a, _OFF_MO, M_WIDTH),
                                kv_group_major(a)], axis=-1)

    def small_cols(a):
        used = jnp.concatenate([gate_slots(a), cols(a, _OFF_MI, M_HEADS), cols(a, _OFF_MF, M_HEADS)], axis=-1)
        pad = [(0, 0)] * (used.ndim - 1) + [(0, _SMALL_WIDTH - used.shape[-1])]
        return jnp.pad(used, pad)

    w_main = main_cols(w_in).astype(BF16)
    b_main = main_cols(b_in)[None, :]
    w_small = small_cols(w_in).astype(BF16)
    b_small = small_cols(b_in)[None, :]

    x1, hn, small = _ffn(x2, p["ffn1_norm"][None, :], p["ffn1_w_gate"].astype(BF16),
                         p["ffn1_w_up"].astype(BF16), p["ffn1_w_down"].astype(BF16),
                         mix=(p["mix_norm"][None, :], w_small, b_small))
    proj = _proj(hn, w_main, b_main)

    rows = s // CMP_STRIDE
    hid = p["cmp_w1_k"].shape[1]
    w1k4 = p["cmp_w1_k"].reshape(2, CMP_STRIDE, dh, hid)
    w1v4 = p["cmp_w1_v"].reshape(2, CMP_STRIDE, dh, hid)
    zeros = jnp.zeros((CMP_STRIDE, dh, 2 * hid), F32)
    wcat = jnp.concatenate([
        jnp.concatenate([w1k4[0], w1k4[1], zeros], axis=-1),
        jnp.concatenate([zeros, w1v4[0], w1v4[1]], axis=-1)], axis=1).astype(BF16)
    kc, vct, ksa, vst, kwn, vwt = _prep(
        proj, wcat,
        p["cmp_pos_k"].reshape(1, CMP_BLOCK * dh), p["cmp_w1_k"].astype(BF16), p["cmp_w2_k"].astype(BF16),
        p["cmp_pos_v"].reshape(1, CMP_BLOCK * dh), p["cmp_w1_v"].astype(BF16), p["cmp_w2_v"].T.astype(BF16),
        p["nsa_kc_gain"][None, :], p["nsa_ks_gain"][None, :], p["nsa_kw_gain"][None, :], batch=b, groups=g)
    c_start = np.arange(rows) * CMP_STRIDE
    b_start = np.arange(dh) * SLC_BLOCK
    ovt = ((c_start[None, :] < (b_start + SLC_BLOCK)[:, None]) & ((c_start + CMP_BLOCK)[None, :] > b_start[:, None])
           & (np.arange(rows) < rows - 1)[None, :])
    o_nsa = _attn(proj, small, kc, vct, ksa, vst, kwn, vwt, p["nsa_q_gain"][:, None],
                  jnp.asarray(ovt, BF16), batch=b, groups=g, tq=_ATTN_TQ)

    gate_t = small[:, _S_MI:_S_MI + 2 * M_HEADS].reshape(b, s // M_CHUNK, M_CHUNK, 2 * M_HEADS)
    gate_t = gate_t.transpose(0, 1, 3, 2)
    upper = np.triu(np.ones((M_CHUNK, M_CHUNK), np.float32))
    cum_u = jnp.asarray(np.stack([upper, upper.T]), BF16)
    h_m = _mlstm(proj, small, gate_t, p["m_conv_w"], p["m_conv_b"][None, :],
                 p["m_out_gain"].reshape(1, M_WIDTH), cum_u, batch=b)

    x3 = _merge(x1, o_nsa, h_m, proj, p["w_branch_nsa"].astype(BF16), p["w_branch_mlstm"].astype(BF16),
                p["w_out"].astype(BF16))
    return _ffn(x3, p["ffn2_norm"][None, :], p["ffn2_w_gate"].astype(BF16),
                p["ffn2_w_up"].astype(BF16), p["ffn2_w_down"].astype(BF16))


def kernel(x, ffn1_norm, ffn1_w_gate, ffn1_w_up, ffn1_w_down, mix_norm, w_in, b_in, nsa_q_gain, nsa_kc_gain, nsa_ks_gain, nsa_kw_gain, cmp_pos_k, cmp_w1_k, cmp_w2_k, cmp_pos_v, cmp_w1_v, cmp_w2_v, m_conv_w, m_conv_b, m_out_gain, w_branch_nsa, w_branch_mlstm, w_out, ffn2_norm, ffn2_w_gate, ffn2_w_up, ffn2_w_down):
    params = dict(ffn1_norm=ffn1_norm, ffn1_w_gate=ffn1_w_gate, ffn1_w_up=ffn1_w_up, ffn1_w_down=ffn1_w_down,
                  mix_norm=mix_norm, w_in=w_in, b_in=b_in, nsa_q_gain=nsa_q_gain, nsa_kc_gain=nsa_kc_gain,
                  nsa_ks_gain=nsa_ks_gain, nsa_kw_gain=nsa_kw_gain, cmp_pos_k=cmp_pos_k, cmp_w1_k=cmp_w1_k,
                  cmp_w2_k=cmp_w2_k, cmp_pos_v=cmp_pos_v, cmp_w1_v=cmp_w1_v, cmp_w2_v=cmp_w2_v,
                  m_conv_w=m_conv_w, m_conv_b=m_conv_b, m_out_gain=m_out_gain, w_branch_nsa=w_branch_nsa,
                  w_branch_mlstm=w_branch_mlstm, w_out=w_out, ffn2_norm=ffn2_norm, ffn2_w_gate=ffn2_w_gate,
                  ffn2_w_up=ffn2_w_up, ffn2_w_down=ffn2_w_down)
    b, s, d = x.shape
    h = x.reshape(b * s, d)
    for l in range(ffn1_norm.shape[0]):
        h = _layer(h, b, s, {k: v[l] for k, v in params.items()})
    return h.reshape(b, s, d)
```

```python
import functools

import numpy as np
import jax
import jax.numpy as jnp
from jax import lax
from jax.experimental import pallas as pl
from jax.experimental.pallas import tpu as pltpu

F32 = jnp.float32
BF16 = jnp.bfloat16

D_MODEL = 2048
D_FF = 5632
NSA_HEADS = 16
NSA_KV_HEADS = 4
NSA_GROUP = NSA_HEADS // NSA_KV_HEADS
NSA_HEAD_DIM = 64
NSA_Q_WIDTH = NSA_HEADS * NSA_HEAD_DIM
NSA_KV_WIDTH = NSA_KV_HEADS * NSA_HEAD_DIM
CMP_BLOCK = 32
CMP_STRIDE = 16
CMP_HIDDEN = 256
SLC_BLOCK = 64
SLC_TOPN = 16
WINDOW = 512
M_HEADS = 4
M_HEAD_DIM = 256
M_WIDTH = M_HEADS * M_HEAD_DIM
M_CHUNK = 64
CONV_WIDTH = 4
EPS = 1e-6
NEG_INF = -1e30
FORCE_SCORE = 1e4
LOG2_E = 1.4426950408889634

_OFF_Q = 0
_OFF_KV = _OFF_Q + NSA_Q_WIDTH
_OFF_G = _OFF_KV + 6 * NSA_KV_WIDTH
_OFF_MQKV = _OFF_G + 3 * NSA_HEADS
_OFF_MI = _OFF_MQKV + 3 * M_WIDTH
_OFF_MF = _OFF_MI + M_HEADS
_OFF_MO = _OFF_MF + M_HEADS
_OFF_MERGE = _OFF_MO + M_WIDTH
_IN_WIDTH = _OFF_MERGE + 2 * D_MODEL

_P_MERGE = 0
_P_Q = 2 * D_MODEL
_P_MQ = _P_Q + NSA_Q_WIDTH
_P_MO = _P_MQ + 3 * M_WIDTH
_P_KV = _P_MO + M_WIDTH
_P_WIDTH = _P_KV + 6 * NSA_KV_WIDTH
_SMALL_WIDTH = 128
_GATE_SLOT = 16
_S_MI = NSA_KV_HEADS * _GATE_SLOT
_S_MF = _S_MI + M_HEADS

VMEM_LIMIT_BYTES = 56 * 1024 * 1024
_ATTN_TQ = 256
_ATTN_PAIR = 2

_NT = (((1,), (1,)), ((), ()))
_TN = (((0,), (0,)), ((), ()))


def _params(*sem):
    return pltpu.CompilerParams(dimension_semantics=sem, vmem_limit_bytes=VMEM_LIMIT_BYTES)


def _dot(a, b):
    return jnp.dot(a, b, preferred_element_type=F32)


def _split3(x):
    hi = x.astype(BF16)
    r1 = x - hi.astype(F32)
    mid = r1.astype(BF16)
    lo = (r1 - mid.astype(F32)).astype(BF16)
    return hi, mid, lo


def _rms(x, gain):
    return x * lax.rsqrt(jnp.mean(x * x, axis=-1, keepdims=True) + EPS) * gain


def _ffn_kernel(*refs, n_ff, emit_mix):
    if emit_mix:
        (x_ref, g_ref, wg_ref, wu_ref, wd_ref, g2_ref, ws_ref, bs_ref,
         o_ref, hn2_ref, small_ref, hn_ref, acc_ref) = refs
    else:
        x_ref, g_ref, wg_ref, wu_ref, wd_ref, o_ref, hn_ref, acc_ref = refs
    j = pl.program_id(1)

    @pl.when(j == 0)
    def _():
        hn_ref[...] = _rms(x_ref[...], g_ref[...]).astype(BF16)
        acc_ref[...] = jnp.zeros_like(acc_ref)

    hn = hn_ref[...]
    a = _dot(hn, wg_ref[...])
    b = _dot(hn, wu_ref[...])
    h = (a * jax.nn.sigmoid(a) * b).astype(BF16)
    acc_ref[...] += _dot(h, wd_ref[...])

    @pl.when(j == n_ff - 1)
    def _():
        y = x_ref[...] + 0.5 * acc_ref[...]
        o_ref[...] = y
        if emit_mix:
            hn2 = _rms(y, g2_ref[...]).astype(BF16)
            hn2_ref[...] = hn2
            small_ref[...] = _dot(hn2, ws_ref[...]) + bs_ref[...]


def _ffn(x, gain, wg, wu, wd, mix=None, *, tm=512, tf=512):
    n, d = x.shape
    f = wg.shape[1]
    assert n % tm == 0 and f % tf == 0
    n_ff = f // tf
    in_specs = [
        pl.BlockSpec((tm, d), lambda i, j: (i, 0)),
        pl.BlockSpec((1, d), lambda i, j: (0, 0)),
        pl.BlockSpec((d, tf), lambda i, j: (0, j)),
        pl.BlockSpec((d, tf), lambda i, j: (0, j)),
        pl.BlockSpec((tf, d), lambda i, j: (j, 0)),
    ]
    out_shape = [jax.ShapeDtypeStruct((n, d), F32)]
    out_specs = [pl.BlockSpec((tm, d), lambda i, j: (i, 0))]
    args = [x, gain, wg, wu, wd]
    if mix is not None:
        g2, ws, bs = mix
        in_specs += [
            pl.BlockSpec((1, d), lambda i, j: (0, 0)),
            pl.BlockSpec((d, _SMALL_WIDTH), lambda i, j: (0, 0)),
            pl.BlockSpec((1, _SMALL_WIDTH), lambda i, j: (0, 0)),
        ]
        out_shape += [jax.ShapeDtypeStruct((n, d), BF16), jax.ShapeDtypeStruct((n, _SMALL_WIDTH), F32)]
        out_specs += [pl.BlockSpec((tm, d), lambda i, j: (i, 0)),
                      pl.BlockSpec((tm, _SMALL_WIDTH), lambda i, j: (i, 0))]
        args += [g2, ws, bs]
    res = pl.pallas_call(
        functools.partial(_ffn_kernel, n_ff=n_ff, emit_mix=mix is not None),
        grid=(n // tm, n_ff),
        in_specs=in_specs,
        out_specs=out_specs,
        out_shape=out_shape,
        scratch_shapes=[pltpu.VMEM((tm, d), BF16), pltpu.VMEM((tm, d), F32)],
        compiler_params=_params("parallel", "arbitrary"),
        name="ffn_mix" if mix is not None else "ffn",
    )(*args)
    return res if mix is not None else res[0]


def _proj_kernel(a_ref, w_ref, b_ref, o_ref):
    o_ref[...] = (_dot(a_ref[...], w_ref[...]) + b_ref[...]).astype(o_ref.dtype)


def _proj(a, w, b, *, tm=1024, tn=1536):
    n, d = a.shape
    p = w.shape[1]
    assert n % tm == 0 and p % tn == 0
    return pl.pallas_call(
        _proj_kernel,
        grid=(p // tn, n // tm),
        in_specs=[
            pl.BlockSpec((tm, d), lambda j, i: (i, 0)),
            pl.BlockSpec((d, tn), lambda j, i: (0, j)),
            pl.BlockSpec((1, tn), lambda j, i: (0, j)),
        ],
        out_specs=pl.BlockSpec((tm, tn), lambda j, i: (i, j)),
        out_shape=jax.ShapeDtypeStruct((n, p), BF16),
        compiler_params=_params("parallel", "parallel"),
        name="proj",
    )(a, w, b)


def _prep_kernel(kv_ref, wcat_ref, pk_ref, w1k_ref, w2k_ref, pv_ref, w1v_ref, w2vt_ref,
                 gc_ref, gs_ref, gw_ref, kc_ref, vct_ref, ksa_ref, vst_ref, kwn_ref, vwt_ref, cmp_ref):
    dh = NSA_HEAD_DIM
    hid = w1k_ref.shape[1]
    rows = kv_ref.shape[0] // CMP_STRIDE

    cmp_ref[...] = kv_ref[:, 0:2 * dh].astype(F32)
    part = None
    for l in range(CMP_STRIDE):
        x_l = cmp_ref[pl.ds(l, rows, stride=CMP_STRIDE), :].astype(BF16)
        term = _dot(x_l, wcat_ref[l])
        part = term if part is None else part + term

    def hidden(first, second, pos_ref, w1_ref):
        pos = jnp.broadcast_to(pos_ref[...], (8, pos_ref.shape[1])).astype(BF16)
        pb = _dot(pos, w1_ref[...])[0:1, :]
        return jax.nn.gelu(first + pltpu.roll(second, rows - 1, axis=0) + pb).astype(BF16)

    hk = hidden(part[:, 0:hid], part[:, hid:2 * hid], pk_ref, w1k_ref)
    hv = hidden(part[:, 2 * hid:3 * hid], part[:, 3 * hid:4 * hid], pv_ref, w1v_ref)
    kc_ref[0] = _rms(_dot(hk, w2k_ref[...]), gc_ref[...]).astype(BF16)
    vct_ref[0] = lax.dot_general(w2vt_ref[...], hv, _NT, preferred_element_type=F32).astype(BF16)

    s = kv_ref.shape[0]
    slc = kv_ref[:, 2 * dh:4 * dh].astype(F32)
    ksn = _rms(slc[:, 0:dh], gs_ref[...]).astype(BF16)
    key_blk = lax.broadcasted_iota(jnp.int32, (s, dh), 0) // SLC_BLOCK
    blk = lax.broadcasted_iota(jnp.int32, (s, dh), 1)
    onehot = (key_blk == blk).astype(BF16)
    ksa_ref[0] = jnp.concatenate([ksn, onehot], axis=1)
    vst_ref[0] = slc.T[dh:2 * dh, :].astype(BF16)
    win = kv_ref[:, 4 * dh:6 * dh].astype(F32)
    kwn_ref[0] = _rms(win[:, 0:dh], gw_ref[...]).astype(BF16)
    vwt_ref[0] = win.T[dh:2 * dh, :].astype(BF16)


def _prep(proj, wcat, pos_k, w1k, w2k, pos_v, w1v, w2vt, gc, gs, gw, *, batch, groups):
    bg = batch * groups
    s = proj.shape[0] // batch
    rows = s // CMP_STRIDE
    dh = NSA_HEAD_DIM
    hid = w1k.shape[1]
    width = CMP_STRIDE * dh
    kv_w = 6 * dh
    kv_col = _P_KV // kv_w
    assert kv_col * kv_w == _P_KV and s % CMP_STRIDE == 0
    full = lambda shape: pl.BlockSpec(shape, lambda i: (0,) * len(shape))
    per = lambda shape: pl.BlockSpec((1,) + shape, lambda i: (i, 0, 0))
    return pl.pallas_call(
        _prep_kernel,
        grid=(bg,),
        in_specs=[pl.BlockSpec((s, kv_w), lambda i: (i // groups, kv_col + i % groups)),
                  full(wcat.shape),
                  full((1, 2 * width)), full((2 * width, hid)), full((hid, dh)),
                  full((1, 2 * width)), full((2 * width, hid)), full((dh, hid)),
                  full((1, dh)), full((1, dh)), full((1, dh))],
        out_specs=[per((rows, dh)), per((dh, rows)), per((s, 2 * dh)), per((dh, s)), per((s, dh)), per((dh, s))],
        out_shape=[jax.ShapeDtypeStruct((bg, rows, dh), BF16), jax.ShapeDtypeStruct((bg, dh, rows), BF16),
                   jax.ShapeDtypeStruct((bg, s, 2 * dh), BF16), jax.ShapeDtypeStruct((bg, dh, s), BF16),
                   jax.ShapeDtypeStruct((bg, s, dh), BF16), jax.ShapeDtypeStruct((bg, dh, s), BF16)],
        scratch_shapes=[pltpu.VMEM((s, 2 * dh), F32)],
        compiler_params=_params("parallel"),
        name="nsa_prep",
    )(proj, wcat, pos_k, w1k, w2k, pos_v, w1v, w2vt, gc, gs, gw)


def _attn_kernel(q_ref, small_ref, kc_ref, vct_ref, ksa_ref, vst_ref, kw_ref, vwt_ref, qg_ref, ovt_ref,
                 o_ref, qa_ref, m_ref, l_ref, acc_ref, ocmp_ref, owin_ref, score_ref, rank_ref, gate_ref,
                 *, tq, pair, groups):
    r = NSA_GROUP
    dh = NSA_HEAD_DIM
    n_cmp = kc_ref.shape[1]
    i = pl.program_id(1)
    q0 = i * tq
    head_lanes = [slice(h * tq, (h + 1) * tq) for h in range(r)]

    q_t = q_ref[...].astype(F32).T
    for h, lanes in enumerate(head_lanes):
        q = q_t[h * dh:(h + 1) * dh, :]
        inv = lax.rsqrt(jnp.mean(q * q, axis=0, keepdims=True) + EPS)
        qa_ref[0:dh, lanes] = (q * inv * qg_ref[...] * (dh ** -0.5 * LOG2_E)).astype(BF16)
    gate_ref[...] = small_ref[...].T
    slot = pl.multiple_of((pl.program_id(0) % groups) * _GATE_SLOT, _GATE_SLOT)
    gt = jax.nn.sigmoid(gate_ref[pl.ds(slot, _GATE_SLOT), :])

    tq_row = q0 + lax.broadcasted_iota(jnp.int32, (1, tq), 1)

    w_start = pl.multiple_of(jnp.maximum(q0 - WINDOW, 0), tq)
    kw_blk = kw_ref[0, pl.ds(w_start, WINDOW + tq), :]
    vw_t = vwt_ref[0, :, pl.ds(w_start, WINDOW + tq)]
    w_back = tq_row - (w_start + lax.broadcasted_iota(jnp.int32, (WINDOW + tq, 1), 0))
    w_mask = (w_back >= 0) & (w_back < WINDOW)
    w_ss = [jnp.where(w_mask, _dot(kw_blk, qa_ref[0:dh, lanes]), NEG_INF) for lanes in head_lanes]
    w_ps = [jnp.exp2(s - jnp.max(s, axis=0, keepdims=True)) for s in w_ss]
    w_os = [_dot(vw_t, p.astype(BF16)) * (1.0 / jnp.sum(p, axis=0, keepdims=True)) for p in w_ps]
    for lanes, o_win in zip(head_lanes, w_os):
        owin_ref[:, lanes] = o_win

    c_end = lax.broadcasted_iota(jnp.int32, (n_cmp, 1), 0) * CMP_STRIDE + (CMP_BLOCK - 1)
    cmask = c_end <= tq_row
    has_cmp = tq_row >= CMP_BLOCK - 1
    scs = [jnp.where(cmask, _dot(kc_ref[0], qa_ref[0:dh, lanes]), NEG_INF) for lanes in head_lanes]
    es = [jnp.exp2(sc - jnp.max(sc, axis=0, keepdims=True)) for sc in scs]
    p_cmps = [e * jnp.where(has_cmp, 1.0 / jnp.sum(e, axis=0, keepdims=True), 0.0) for e in es]
    o_cmps = [_dot(vct_ref[0], p_cmp.astype(BF16)) for p_cmp in p_cmps]
    for lanes, o_cmp in zip(head_lanes, o_cmps):
        ocmp_ref[:, lanes] = o_cmp
    p_sum = functools.reduce(jnp.add, p_cmps)
    ovt = ovt_ref[...]
    imp_t = sum(_dot(ovt, part) for part in _split3(p_sum))
    n_blk = imp_t.shape[0]
    blk = lax.broadcasted_iota(jnp.int32, (n_blk, 1), 0)
    cur = tq_row // SLC_BLOCK
    forced = (blk == 0) | (blk == cur) | (blk == cur - 1)
    score_ref[...] = jnp.where(forced, FORCE_SCORE, jnp.where(blk * SLC_BLOCK <= tq_row, imp_t, -1.0))
    rank_ref[...] = jnp.zeros_like(rank_ref)
    sub = 8
    last_blk = (q0 + tq - 1) // SLC_BLOCK
    for kg in range(0, n_blk, sub):
        @pl.when(kg <= last_blk)
        def _(kg=kg):
            for ng in range(0, n_blk, sub):
                sn = score_ref[ng:ng + sub, :]
                rn = rank_ref[ng:ng + sub, :]
                for k in range(kg, kg + sub):
                    sk = score_ref[k:k + 1, :]
                    if ng > kg:
                        ahead = sk >= sn
                    elif ng < kg:
                        ahead = sk > sn
                    else:
                        later = lax.broadcasted_iota(jnp.int32, (sub, 1), 0) > (k - kg)
                        ahead = (sk > sn) | (later & (sk == sn))
                    rn = jnp.where(ahead, rn + 1.0, rn)
                rank_ref[ng:ng + sub, :] = rn
    sel_bias = jnp.where(rank_ref[...] < SLC_TOPN, 0.0, NEG_INF).astype(BF16)
    for lanes in head_lanes:
        qa_ref[dh:2 * dh, lanes] = sel_bias

    def key_block(k_ref, vt_ref, kdim, k0, n_keys, window, init):
        start = pl.multiple_of(k0, tq)
        k_blk = k_ref[0, pl.ds(start, n_keys), :]
        vt = vt_ref[0, :, pl.ds(start, n_keys)]
        ss = [_dot(k_blk, qa_ref[0:kdim, lanes]) for lanes in head_lanes]
        if window is not None:
            back = tq_row - (start + lax.broadcasted_iota(jnp.int32, (n_keys, 1), 0))
            mask = back >= 0
            if window != float("inf"):
                mask = mask & (back < window)
            ss = [jnp.where(mask, s, NEG_INF) for s in ss]
        mxs = [jnp.max(s, axis=0, keepdims=True) for s in ss]
        if init:
            ps = [jnp.exp2(s - mx) for s, mx in zip(ss, mxs)]
            for lanes, mx, p in zip(head_lanes, mxs, ps):
                m_ref[:, lanes] = mx
                l_ref[:, lanes] = jnp.sum(p, axis=0, keepdims=True)
                acc_ref[:, lanes] = _dot(vt, p.astype(BF16))
        else:
            m_prevs = [m_ref[:, lanes] for lanes in head_lanes]
            m_news = [jnp.maximum(mp, mx) for mp, mx in zip(m_prevs, mxs)]
            alphas = [jnp.exp2(mp - mn) for mp, mn in zip(m_prevs, m_news)]
            ps = [jnp.exp2(s - mn) for s, mn in zip(ss, m_news)]
            pvs = [_dot(vt, p.astype(BF16)) for p in ps]
            for lanes, mn, al, p, pv in zip(head_lanes, m_news, alphas, ps, pvs):
                m_ref[:, lanes] = mn
                l_ref[:, lanes] = al * l_ref[:, lanes] + jnp.sum(p, axis=0, keepdims=True)
                acc_ref[:, lanes] = al * acc_ref[:, lanes] + pv

    for rem in range(pair):
        @pl.when(i % pair == rem)
        def _(rem=rem):
            key_block(ksa_ref, vst_ref, 2 * dh, (i - rem) * tq, (rem + 1) * tq, float("inf"), True)

    def slc_body(kt, carry):
        key_block(ksa_ref, vst_ref, 2 * dh, kt * (pair * tq), pair * tq, None, False)
        return carry

    lax.fori_loop(0, i // pair, slc_body, 0)
    for h, lanes in enumerate(head_lanes):
        ocmp_ref[:, lanes] = (gt[3 * h:3 * h + 1, :] * ocmp_ref[:, lanes]
                              + gt[3 * h + 1:3 * h + 2, :] * (acc_ref[:, lanes] / l_ref[:, lanes]))

    outs = [ocmp_ref[:, lanes] + gt[3 * h + 2:3 * h + 3, :] * owin_ref[:, lanes]
            for h, lanes in enumerate(head_lanes)]
    o_ref[...] = jnp.concatenate(outs, axis=0).T.astype(o_ref.dtype)


def _attn(proj, small, kc, vct, ksa, vst, kwn, vwt, q_gain_col, ovt, *, batch, groups, tq):
    n = proj.shape[0]
    s = n // batch
    nq = s // tq
    r, dh = NSA_GROUP, NSA_HEAD_DIM
    m_cols = r * tq
    q_col = _P_Q // (r * dh)
    assert q_col * r * dh == _P_Q and s % tq == 0
    assert WINDOW % tq == 0 and s >= WINDOW + tq and s // SLC_BLOCK <= dh and tq % SLC_BLOCK == 0
    assert groups * _GATE_SLOT <= _SMALL_WIDTH and 3 * r <= _GATE_SLOT
    n_cmp = kc.shape[1]
    per_bg = lambda shape: pl.BlockSpec((1,) + shape, lambda bg, i: (bg, 0, 0))
    return pl.pallas_call(
        functools.partial(_attn_kernel, tq=tq, pair=_ATTN_PAIR, groups=groups),
        grid=(batch * groups, nq),
        in_specs=[
            pl.BlockSpec((tq, r * dh), lambda bg, i: ((bg // groups) * nq + i, q_col + bg % groups)),
            pl.BlockSpec((tq, _SMALL_WIDTH), lambda bg, i: ((bg // groups) * nq + i, 0)),
            per_bg((n_cmp, dh)), per_bg((dh, n_cmp)),
            per_bg((s, 2 * dh)), per_bg((dh, s)), per_bg((s, dh)), per_bg((dh, s)),
            pl.BlockSpec((dh, 1), lambda bg, i: (0, 0)),
            pl.BlockSpec(ovt.shape, lambda bg, i: (0, 0)),
        ],
        out_specs=pl.BlockSpec((tq, r * dh), lambda bg, i: ((bg // groups) * nq + i, bg % groups)),
        out_shape=jax.ShapeDtypeStruct((n, groups * r * dh), BF16),
        scratch_shapes=[pltpu.VMEM((2 * dh, m_cols), BF16),
                        pltpu.VMEM((1, m_cols), F32), pltpu.VMEM((1, m_cols), F32),
                        pltpu.VMEM((dh, m_cols), F32), pltpu.VMEM((dh, m_cols), F32),
                        pltpu.VMEM((dh, m_cols), F32),
                        pltpu.VMEM((dh, tq), F32), pltpu.VMEM((dh, tq), F32),
                        pltpu.VMEM((_SMALL_WIDTH, tq), F32)],
        compiler_params=_params("parallel", "parallel"),
        name="nsa_attn",
    )(proj, small, kc, vct, ksa, vst, kwn, vwt, q_gain_col, ovt)


def _mlstm_kernel(mq_ref, mk_ref, mv_ref, mo_ref, small_ref, gt_ref, cw_ref, cb_ref, og_ref, cum_ref,
                  o_ref, xbuf_ref, c_ref, n_ref, m_ref, *, tt):
    hd = M_HEAD_DIM
    L = M_CHUNK
    width = M_WIDTH
    n_chunk = tt // L
    j = pl.program_id(1)

    @pl.when(j == 0)
    def _():
        xbuf_ref[0:8, :] = jnp.zeros((8, 2 * width), F32)
        c_ref[...] = jnp.zeros_like(c_ref)
        n_ref[...] = jnp.zeros_like(n_ref)
        m_ref[...] = jnp.zeros_like(m_ref)

    @pl.when(j > 0)
    def _():
        xbuf_ref[0:8, :] = xbuf_ref[tt:tt + 8, :]

    xbuf_ref[8:8 + tt, 0:width] = mq_ref[...].astype(F32)
    xbuf_ref[8:8 + tt, width:2 * width] = mk_ref[...].astype(F32)
    conv = cb_ref[...]
    for tap in range(CONV_WIDTH):
        off = 8 - (CONV_WIDTH - 1) + tap
        conv = conv + cw_ref[tap:tap + 1, :] * xbuf_ref[off:off + tt, :]
    qk = conv * jax.nn.sigmoid(conv)
    q_all = qk[:, 0:width]
    k_all = qk[:, width:2 * width] * (hd ** -0.5)

    small = small_ref[...]
    logf_col = jax.nn.log_sigmoid(small)
    gt = gt_ref[0].reshape(n_chunk * 8, L)
    logf_row = jax.nn.log_sigmoid(gt)
    cum_u = cum_ref[0]
    cum_l = cum_ref[1]
    a_row_all = sum(_dot(part, cum_u) for part in _split3(logf_row))
    causal = lax.broadcasted_iota(jnp.int32, (L, L), 1) <= lax.broadcasted_iota(jnp.int32, (L, L), 0)

    for c in range(n_chunk):
        rows = slice(c * L, (c + 1) * L)
        a_col_c = sum(_dot(cum_l, part) for part in _split3(logf_col[rows]))
        pending = []
        for h in range(M_HEADS):
            lanes = slice(h * hd, (h + 1) * hd)
            q_c = q_all[rows, lanes]
            k_c = k_all[rows, lanes]
            v_c = mv_ref[rows, lanes]
            q_b = q_c.astype(BF16)
            a_j = a_col_c[:, _S_MF + h:_S_MF + h + 1]
            li_j = small[rows, _S_MI + h:_S_MI + h + 1]
            a_s = a_row_all[c * 8 + M_HEADS + h:c * 8 + M_HEADS + h + 1, :]
            li_s = gt[c * 8 + h:c * 8 + h + 1, :]
            g_c = a_j[L - 1:L, :]
            m_st = m_ref[h:h + 1, 0:1]
            c_st = c_ref[h]
            n_st = n_ref[h]

            log_w = jnp.where(causal, a_j - a_s + li_s, NEG_INF)
            m_intra = jnp.max(log_w, axis=-1, keepdims=True)
            w = jnp.exp(log_w - m_intra) * lax.dot_general(q_b, k_c.astype(BF16), _NT, preferred_element_type=F32)
            num_intra = _dot(w.astype(BF16), v_c)
            den_intra = jnp.sum(w, axis=-1, keepdims=True)

            num_inter = _dot(q_b, c_st.astype(BF16))
            den_inter = jnp.sum(q_c * n_st, axis=-1, keepdims=True)

            log_inter = a_j + m_st
            m_comb = jnp.maximum(log_inter, m_intra)
            s_inter = jnp.exp(log_inter - m_comb)
            s_intra = jnp.exp(m_intra - m_comb)
            num = s_inter * num_inter + s_intra * num_intra
            den = s_inter * den_inter + s_intra * den_intra
            hcell = num / jnp.maximum(jnp.abs(den), jnp.exp(-m_comb))

            log_u = g_c - a_j + li_j
            m_new = jnp.maximum(g_c + m_st, jnp.max(log_u, axis=0, keepdims=True))
            decay = jnp.exp(g_c + m_st - m_new)
            uk = jnp.exp(log_u - m_new) * k_c
            c_new = decay * c_st + lax.dot_general(uk.astype(BF16), v_c, _TN, preferred_element_type=F32)
            n_new = decay * n_st + jnp.sum(uk, axis=0, keepdims=True)

            hn = _rms(hcell, og_ref[:, lanes])
            o_gate = jax.nn.sigmoid(mo_ref[rows, lanes].astype(F32))
            pending.append((h, lanes, c_new, n_new, m_new, (o_gate * hn).astype(o_ref.dtype)))
        for h, lanes, c_new, n_new, m_new, out in pending:
            c_ref[h] = c_new
            n_ref[h] = n_new
            m_ref[h:h + 1, :] = jnp.broadcast_to(m_new, (1, m_ref.shape[1]))
            o_ref[rows, lanes] = out


def _mlstm(proj, small, gate_t, conv_w, conv_b, out_gain, cum_u, *, batch, tt=256):
    n = proj.shape[0]
    s = n // batch
    assert s % tt == 0 and tt % M_CHUNK == 0
    nt = s // tt
    cq = _P_MQ // M_WIDTH
    assert cq * M_WIDTH == _P_MQ
    col = lambda k: pl.BlockSpec((tt, M_WIDTH), lambda b, j: (b * nt + j, k))
    full = lambda shape: pl.BlockSpec(shape, lambda b, j: (0,) * len(shape))
    return pl.pallas_call(
        functools.partial(_mlstm_kernel, tt=tt),
        grid=(batch, nt),
        in_specs=[col(cq), col(cq + 1), col(cq + 2), col(cq + 3),
                  pl.BlockSpec((tt, _SMALL_WIDTH), lambda b, j: (b * nt + j, 0)),
                  pl.BlockSpec((1, tt // M_CHUNK, 8, M_CHUNK), lambda b, j: (b, j, 0, 0)),
                  full(conv_w.shape), full(conv_b.shape), full(out_gain.shape), full(cum_u.shape)],
        out_specs=pl.BlockSpec((tt, M_WIDTH), lambda b, j: (b * nt + j, 0)),
        out_shape=jax.ShapeDtypeStruct((n, M_WIDTH), BF16),
        scratch_shapes=[pltpu.VMEM((tt + 8, 2 * M_WIDTH), F32),
                        pltpu.VMEM((M_HEADS, M_HEAD_DIM, M_HEAD_DIM), F32),
                        pltpu.VMEM((M_HEADS, 1, M_HEAD_DIM), F32),
                        pltpu.VMEM((8, 128), F32)],
        compiler_params=_params("parallel", "arbitrary"),
        name="mlstm",
    )(proj, proj, proj, proj, small, gate_t, conv_w, conv_b, out_gain, cum_u)


def _merge_kernel(x_ref, oa_ref, ob_ref, ga_ref, gb_ref, wa_ref, wb_ref, wo_ref, o_ref):
    ya = _dot(oa_ref[...], wa_ref[...])
    yb = _dot(ob_ref[...], wb_ref[...])
    merged = (jax.nn.sigmoid(ga_ref[...].astype(F32)) * ya
              + jax.nn.sigmoid(gb_ref[...].astype(F32)) * yb)
    o_ref[...] = x_ref[...] + _dot(merged.astype(BF16), wo_ref[...])


def _merge(x, o_nsa, h_m, proj, wa, wb, wo, *, tm=256):
    n, d = x.shape
    assert n % tm == 0 and _P_MERGE == 0
    const = lambda shape: pl.BlockSpec(shape, lambda i: (0, 0), pipeline_mode=pl.Buffered(1))
    return pl.pallas_call(
        _merge_kernel,
        grid=(n // tm,),
        in_specs=[
            pl.BlockSpec((tm, d), lambda i: (i, 0)),
            pl.BlockSpec((tm, o_nsa.shape[1]), lambda i: (i, 0)),
            pl.BlockSpec((tm, h_m.shape[1]), lambda i: (i, 0)),
            pl.BlockSpec((tm, d), lambda i: (i, 0)),
            pl.BlockSpec((tm, d), lambda i: (i, 1)),
            const(wa.shape), const(wb.shape), const(wo.shape),
        ],
        out_specs=pl.BlockSpec((tm, d), lambda i: (i, 0)),
        out_shape=jax.ShapeDtypeStruct((n, d), F32),
        compiler_params=_params("parallel"),
        name="merge",
    )(x, o_nsa, h_m, proj, proj, wa, wb, wo)


def _layer(x2, b, s, p):
    n = x2.shape[0]
    g, r, dh = NSA_KV_HEADS, NSA_GROUP, NSA_HEAD_DIM
    w_in, b_in = p["w_in"], p["b_in"]

    def cols(a, start, size):
        return a[..., start:start + size]

    def kv_group_major(a):
        kv = cols(a, _OFF_KV, 6 * NSA_KV_WIDTH)
        kv = kv.reshape(a.shape[:-1] + (6, g, dh))
        return jnp.swapaxes(kv, -3, -2).reshape(a.shape[:-1] + (6 * NSA_KV_WIDTH,))

    def gate_slots(a):
        gates = cols(a, _OFF_G, 3 * NSA_HEADS).reshape(a.shape[:-1] + (g, 3 * r))
        pad = [(0, 0)] * (gates.ndim - 1) + [(0, _GATE_SLOT - 3 * r)]
        return jnp.pad(gates, pad).reshape(a.shape[:-1] + (g * _GATE_SLOT,))

    def main_cols(a):
        return jnp.concatenate([cols(a, _OFF_MERGE, 2 * D_MODEL), cols(a, _OFF_Q, NSA_Q_WIDTH),
                                cols(a, _OFF_MQKV, 3 * M_WIDTH), cols(a, _OFF_MO, M_WIDTH),
                                kv_group_major(a)], axis=-1)

    def small_cols(a):
        used = jnp.concatenate([gate_slots(a), cols(a, _OFF_MI, M_HEADS), cols(a, _OFF_MF, M_HEADS)], axis=-1)
        pad = [(0, 0)] * (used.ndim - 1) + [(0, _SMALL_WIDTH - used.shape[-1])]
        return jnp.pad(used, pad)

    w_main = main_cols(w_in).astype(BF16)
    b_main = main_cols(b_in)[None, :]
    w_small = small_cols(w_in).astype(BF16)
    b_small = small_cols(b_in)[None, :]

    x1, hn, small = _ffn(x2, p["ffn1_norm"][None, :], p["ffn1_w_gate"].astype(BF16),
                         p["ffn1_w_up"].astype(BF16), p["ffn1_w_down"].astype(BF16),
                         mix=(p["mix_norm"][None, :], w_small, b_small))
    proj = _proj(hn, w_main, b_main)

    rows = s // CMP_STRIDE
    hid = p["cmp_w1_k"].shape[1]
    w1k4 = p["cmp_w1_k"].reshape(2, CMP_STRIDE, dh, hid)
    w1v4 = p["cmp_w1_v"].reshape(2, CMP_STRIDE, dh, hid)
    zeros = jnp.zeros((CMP_STRIDE, dh, 2 * hid), F32)
    wcat = jnp.concatenate([
        jnp.concatenate([w1k4[0], w1k4[1], zeros], axis=-1),
        jnp.concatenate([zeros, w1v4[0], w1v4[1]], axis=-1)], axis=1).astype(BF16)
    kc, vct, ksa, vst, kwn, vwt = _prep(
        proj, wcat,
        p["cmp_pos_k"].reshape(1, CMP_BLOCK * dh), p["cmp_w1_k"].astype(BF16), p["cmp_w2_k"].astype(BF16),
        p["cmp_pos_v"].reshape(1, CMP_BLOCK * dh), p["cmp_w1_v"].astype(BF16), p["cmp_w2_v"].T.astype(BF16),
        p["nsa_kc_gain"][None, :], p["nsa_ks_gain"][None, :], p["nsa_kw_gain"][None, :], batch=b, groups=g)
    c_start = np.arange(rows) * CMP_STRIDE
    b_start = np.arange(dh) * SLC_BLOCK
    ovt = ((c_start[None, :] < (b_start + SLC_BLOCK)[:, None]) & ((c_start + CMP_BLOCK)[None, :] > b_start[:, None])
           & (np.arange(rows) < rows - 1)[None, :])
    o_nsa = _attn(proj, small, kc, vct, ksa, vst, kwn, vwt, p["nsa_q_gain"][:, None],
                  jnp.asarray(ovt, BF16), batch=b, groups=g, tq=_ATTN_TQ)

    gate_t = small[:, _S_MI:_S_MI + 2 * M_HEADS].reshape(b, s // M_CHUNK, M_CHUNK, 2 * M_HEADS)
    gate_t = gate_t.transpose(0, 1, 3, 2)
    upper = np.triu(np.ones((M_CHUNK, M_CHUNK), np.float32))
    cum_u = jnp.asarray(np.stack([upper, upper.T]), BF16)
    h_m = _mlstm(proj, small, gate_t, p["m_conv_w"], p["m_conv_b"][None, :],
                 p["m_out_gain"].reshape(1, M_WIDTH), cum_u, batch=b)

    x3 = _merge(x1, o_nsa, h_m, proj, p["w_branch_nsa"].astype(BF16), p["w_branch_mlstm"].astype(BF16),
                p["w_out"].astype(BF16))
    return _ffn(x3, p["ffn2_norm"][None, :], p["ffn2_w_gate"].astype(BF16),
                p["ffn2_w_up"].astype(BF16), p["ffn2_w_down"].astype(BF16))


def kernel(x, ffn1_norm, ffn1_w_gate, ffn1_w_up, ffn1_w_down, mix_norm, w_in, b_in, nsa_q_gain, nsa_kc_gain, nsa_ks_gain, nsa_kw_gain, cmp_pos_k, cmp_w1_k, cmp_w2_k, cmp_pos_v, cmp_w1_v, cmp_w2_v, m_conv_w, m_conv_b, m_out_gain, w_branch_nsa, w_branch_mlstm, w_out, ffn2_norm, ffn2_w_gate, ffn2_w_up, ffn2_w_down):
    params = dict(ffn1_norm=ffn1_norm, ffn1_w_gate=ffn1_w_gate, ffn1_w_up=ffn1_w_up, ffn1_w_down=ffn1_w_down,
                  mix_norm=mix_norm, w_in=w_in, b_in=b_in, nsa_q_gain=nsa_q_gain, nsa_kc_gain=nsa_kc_gain,
                  nsa_ks_gain=nsa_ks_gain, nsa_kw_gain=nsa_kw_gain, cmp_pos_k=cmp_pos_k, cmp_w1_k=cmp_w1_k,
                  cmp_w2_k=cmp_w2_k, cmp_pos_v=cmp_pos_v, cmp_w1_v=cmp_w1_v, cmp_w2_v=cmp_w2_v,
                  m_conv_w=m_conv_w, m_conv_b=m_conv_b, m_out_gain=m_out_gain, w_branch_nsa=w_branch_nsa,
                  w_branch_mlstm=w_branch_mlstm, w_out=w_out, ffn2_norm=ffn2_norm, ffn2_w_gate=ffn2_w_gate,
                  ffn2_w_up=ffn2_w_up, ffn2_w_down=ffn2_w_down)
    b, s, d = x.shape
    h = x.reshape(b * s, d)
    for l in range(ffn1_norm.shape[0]):
        h = _layer(h, b, s, {k: v[l] for k, v in params.items()})
    return h.reshape(b, s, d)
```

```python
import functools

import numpy as np
import jax
import jax.numpy as jnp
from jax import lax
from jax.experimental import pallas as pl
from jax.experimental.pallas import tpu as pltpu

F32 = jnp.float32
BF16 = jnp.bfloat16

D_MODEL = 2048
D_FF = 5632
NSA_HEADS = 16
NSA_KV_HEADS = 4
NSA_GROUP = NSA_HEADS // NSA_KV_HEADS
NSA_HEAD_DIM = 64
NSA_Q_WIDTH = NSA_HEADS * NSA_HEAD_DIM
NSA_KV_WIDTH = NSA_KV_HEADS * NSA_HEAD_DIM
CMP_BLOCK = 32
CMP_STRIDE = 16
CMP_HIDDEN = 256
SLC_BLOCK = 64
SLC_TOPN = 16
WINDOW = 512
M_HEADS = 4
M_HEAD_DIM = 256
M_WIDTH = M_HEADS * M_HEAD_DIM
M_CHUNK = 64
CONV_WIDTH = 4
EPS = 1e-6
NEG_INF = -1e30
FORCE_SCORE = 1e4
LOG2_E = 1.4426950408889634

_OFF_Q = 0
_OFF_KV = _OFF_Q + NSA_Q_WIDTH
_OFF_G = _OFF_KV + 6 * NSA_KV_WIDTH
_OFF_MQKV = _OFF_G + 3 * NSA_HEADS
_OFF_MI = _OFF_MQKV + 3 * M_WIDTH
_OFF_MF = _OFF_MI + M_HEADS
_OFF_MO = _OFF_MF + M_HEADS
_OFF_MERGE = _OFF_MO + M_WIDTH
_IN_WIDTH = _OFF_MERGE + 2 * D_MODEL

_P_MERGE = 0
_P_Q = 2 * D_MODEL
_P_MQ = _P_Q + NSA_Q_WIDTH
_P_MO = _P_MQ + 3 * M_WIDTH
_P_KV = _P_MO + M_WIDTH
_P_WIDTH = _P_KV + 6 * NSA_KV_WIDTH
_SMALL_WIDTH = 128
_GATE_SLOT = 16
_S_MI = NSA_KV_HEADS * _GATE_SLOT
_S_MF = _S_MI + M_HEADS

VMEM_LIMIT_BYTES = 56 * 1024 * 1024
_ATTN_TQ = 256
_ATTN_PAIR = 4

_NT = (((1,), (1,)), ((), ()))
_TN = (((0,), (0,)), ((), ()))


def _params(*sem):
    return pltpu.CompilerParams(dimension_semantics=sem, vmem_limit_bytes=VMEM_LIMIT_BYTES)


def _dot(a, b):
    return jnp.dot(a, b, preferred_element_type=F32)


def _split3(x):
    hi = x.astype(BF16)
    r1 = x - hi.astype(F32)
    mid = r1.astype(BF16)
    lo = (r1 - mid.astype(F32)).astype(BF16)
    return hi, mid, lo


def _rms(x, gain):
    return x * lax.rsqrt(jnp.mean(x * x, axis=-1, keepdims=True) + EPS) * gain


def _ffn_kernel(*refs, n_ff, emit_mix):
    if emit_mix:
        (x_ref, g_ref, wg_ref, wu_ref, wd_ref, g2_ref, ws_ref, bs_ref,
         o_ref, hn2_ref, small_ref, hn_ref, acc_ref) = refs
    else:
        x_ref, g_ref, wg_ref, wu_ref, wd_ref, o_ref, hn_ref, acc_ref = refs
    j = pl.program_id(1)

    @pl.when(j == 0)
    def _():
        hn_ref[...] = _rms(x_ref[...], g_ref[...]).astype(BF16)
        acc_ref[...] = jnp.zeros_like(acc_ref)

    hn = hn_ref[...]
    a = _dot(hn, wg_ref[...])
    b = _dot(hn, wu_ref[...])
    h = (a * jax.nn.sigmoid(a) * b).astype(BF16)
    acc_ref[...] += _dot(h, wd_ref[...])

    @pl.when(j == n_ff - 1)
    def _():
        y = x_ref[...] + 0.5 * acc_ref[...]
        o_ref[...] = y
        if emit_mix:
            hn2 = _rms(y, g2_ref[...]).astype(BF16)
            hn2_ref[...] = hn2
            small_ref[...] = _dot(hn2, ws_ref[...]) + bs_ref[...]


def _ffn(x, gain, wg, wu, wd, mix=None, *, tm=512, tf=512):
    n, d = x.shape
    f = wg.shape[1]
    assert n % tm == 0 and f % tf == 0
    n_ff = f // tf
    in_specs = [
        pl.BlockSpec((tm, d), lambda i, j: (i, 0)),
        pl.BlockSpec((1, d), lambda i, j: (0, 0)),
        pl.BlockSpec((d, tf), lambda i, j: (0, j)),
        pl.BlockSpec((d, tf), lambda i, j: (0, j)),
        pl.BlockSpec((tf, d), lambda i, j: (j, 0)),
    ]
    out_shape = [jax.ShapeDtypeStruct((n, d), F32)]
    out_specs = [pl.BlockSpec((tm, d), lambda i, j: (i, 0))]
    args = [x, gain, wg, wu, wd]
    if mix is not None:
        g2, ws, bs = mix
        in_specs += [
            pl.BlockSpec((1, d), lambda i, j: (0, 0)),
            pl.BlockSpec((d, _SMALL_WIDTH), lambda i, j: (0, 0)),
            pl.BlockSpec((1, _SMALL_WIDTH), lambda i, j: (0, 0)),
        ]
        out_shape += [jax.ShapeDtypeStruct((n, d), BF16), jax.ShapeDtypeStruct((n, _SMALL_WIDTH), F32)]
        out_specs += [pl.BlockSpec((tm, d), lambda i, j: (i, 0)),
                      pl.BlockSpec((tm, _SMALL_WIDTH), lambda i, j: (i, 0))]
        args += [g2, ws, bs]
    res = pl.pallas_call(
        functools.partial(_ffn_kernel, n_ff=n_ff, emit_mix=mix is not None),
        grid=(n // tm, n_ff),
        in_specs=in_specs,
        out_specs=out_specs,
        out_shape=out_shape,
        scratch_shapes=[pltpu.VMEM((tm, d), BF16), pltpu.VMEM((tm, d), F32)],
        compiler_params=_params("parallel", "arbitrary"),
        name="ffn_mix" if mix is not None else "ffn",
    )(*args)
    return res if mix is not None else res[0]


def _proj_kernel(a_ref, w_ref, b_ref, o_ref):
    o_ref[...] = (_dot(a_ref[...], w_ref[...]) + b_ref[...]).astype(o_ref.dtype)


def _proj(a, w, b, *, tm=1024, tn=1536):
    n, d = a.shape
    p = w.shape[1]
    assert n % tm == 0 and p % tn == 0
    return pl.pallas_call(
        _proj_kernel,
        grid=(p // tn, n // tm),
        in_specs=[
            pl.BlockSpec((tm, d), lambda j, i: (i, 0)),
            pl.BlockSpec((d, tn), lambda j, i: (0, j)),
            pl.BlockSpec((1, tn), lambda j, i: (0, j)),
        ],
        out_specs=pl.BlockSpec((tm, tn), lambda j, i: (i, j)),
        out_shape=jax.ShapeDtypeStruct((n, p), BF16),
        compiler_params=_params("parallel", "parallel"),
        name="proj",
    )(a, w, b)


def _prep_kernel(kv_ref, wcat_ref, pk_ref, w1k_ref, w2k_ref, pv_ref, w1v_ref, w2vt_ref,
                 gc_ref, gs_ref, gw_ref, kc_ref, vct_ref, ksa_ref, vst_ref, kwn_ref, vwt_ref, cmp_ref):
    dh = NSA_HEAD_DIM
    hid = w1k_ref.shape[1]
    rows = kv_ref.shape[0] // CMP_STRIDE

    cmp_ref[...] = kv_ref[:, 0:2 * dh].astype(F32)
    part = None
    for l in range(CMP_STRIDE):
        x_l = cmp_ref[pl.ds(l, rows, stride=CMP_STRIDE), :].astype(BF16)
        term = _dot(x_l, wcat_ref[l])
        part = term if part is None else part + term

    def hidden(first, second, pos_ref, w1_ref):
        pos = jnp.broadcast_to(pos_ref[...], (8, pos_ref.shape[1])).astype(BF16)
        pb = _dot(pos, w1_ref[...])[0:1, :]
        return jax.nn.gelu(first + pltpu.roll(second, rows - 1, axis=0) + pb).astype(BF16)

    hk = hidden(part[:, 0:hid], part[:, hid:2 * hid], pk_ref, w1k_ref)
    hv = hidden(part[:, 2 * hid:3 * hid], part[:, 3 * hid:4 * hid], pv_ref, w1v_ref)
    kc_ref[0] = _rms(_dot(hk, w2k_ref[...]), gc_ref[...]).astype(BF16)
    vct_ref[0] = lax.dot_general(w2vt_ref[...], hv, _NT, preferred_element_type=F32).astype(BF16)

    s = kv_ref.shape[0]
    slc = kv_ref[:, 2 * dh:4 * dh].astype(F32)
    ksn = _rms(slc[:, 0:dh], gs_ref[...]).astype(BF16)
    key_blk = lax.broadcasted_iota(jnp.int32, (s, dh), 0) // SLC_BLOCK
    blk = lax.broadcasted_iota(jnp.int32, (s, dh), 1)
    onehot = (key_blk == blk).astype(BF16)
    ksa_ref[0] = jnp.concatenate([ksn, onehot], axis=1)
    vst_ref[0] = slc.T[dh:2 * dh, :].astype(BF16)
    win = kv_ref[:, 4 * dh:6 * dh].astype(F32)
    kwn_ref[0] = _rms(win[:, 0:dh], gw_ref[...]).astype(BF16)
    vwt_ref[0] = win.T[dh:2 * dh, :].astype(BF16)


def _prep(proj, wcat, pos_k, w1k, w2k, pos_v, w1v, w2vt, gc, gs, gw, *, batch, groups):
    bg = batch * groups
    s = proj.shape[0] // batch
    rows = s // CMP_STRIDE
    dh = NSA_HEAD_DIM
    hid = w1k.shape[1]
    width = CMP_STRIDE * dh
    kv_w = 6 * dh
    kv_col = _P_KV // kv_w
    assert kv_col * kv_w == _P_KV and s % CMP_STRIDE == 0
    full = lambda shape: pl.BlockSpec(shape, lambda i: (0,) * len(shape))
    per = lambda shape: pl.BlockSpec((1,) + shape, lambda i: (i, 0, 0))
    return pl.pallas_call(
        _prep_kernel,
        grid=(bg,),
        in_specs=[pl.BlockSpec((s, kv_w), lambda i: (i // groups, kv_col + i % groups)),
                  full(wcat.shape),
                  full((1, 2 * width)), full((2 * width, hid)), full((hid, dh)),
                  full((1, 2 * width)), full((2 * width, hid)), full((dh, hid)),
                  full((1, dh)), full((1, dh)), full((1, dh))],
        out_specs=[per((rows, dh)), per((dh, rows)), per((s, 2 * dh)), per((dh, s)), per((s, dh)), per((dh, s))],
        out_shape=[jax.ShapeDtypeStruct((bg, rows, dh), BF16), jax.ShapeDtypeStruct((bg, dh, rows), BF16),
                   jax.ShapeDtypeStruct((bg, s, 2 * dh), BF16), jax.ShapeDtypeStruct((bg, dh, s), BF16),
                   jax.ShapeDtypeStruct((bg, s, dh), BF16), jax.ShapeDtypeStruct((bg, dh, s), BF16)],
        scratch_shapes=[pltpu.VMEM((s, 2 * dh), F32)],
        compiler_params=_params("parallel"),
        name="nsa_prep",
    )(proj, wcat, pos_k, w1k, w2k, pos_v, w1v, w2vt, gc, gs, gw)


def _attn_kernel(q_ref, small_ref, kc_ref, vct_ref, ksa_ref, vst_ref, kw_ref, vwt_ref, qg_ref, ovt_ref,
                 o_ref, qa_ref, m_ref, l_ref, acc_ref, ocmp_ref, owin_ref, score_ref, rank_ref, gate_ref,
                 *, tq, pair, groups):
    r = NSA_GROUP
    dh = NSA_HEAD_DIM
    n_cmp = kc_ref.shape[1]
    i = pl.program_id(1)
    q0 = i * tq
    head_lanes = [slice(h * tq, (h + 1) * tq) for h in range(r)]

    q_t = q_ref[...].astype(F32).T
    for h, lanes in enumerate(head_lanes):
        q = q_t[h * dh:(h + 1) * dh, :]
        inv = lax.rsqrt(jnp.mean(q * q, axis=0, keepdims=True) + EPS)
        qa_ref[0:dh, lanes] = (q * inv * qg_ref[...] * (dh ** -0.5 * LOG2_E)).astype(BF16)
    gate_ref[...] = small_ref[...].T
    slot = pl.multiple_of((pl.program_id(0) % groups) * _GATE_SLOT, _GATE_SLOT)
    gt = jax.nn.sigmoid(gate_ref[pl.ds(slot, _GATE_SLOT), :])

    tq_row = q0 + lax.broadcasted_iota(jnp.int32, (1, tq), 1)

    w_start = pl.multiple_of(jnp.maximum(q0 - WINDOW, 0), tq)
    kw_blk = kw_ref[0, pl.ds(w_start, WINDOW + tq), :]
    vw_t = vwt_ref[0, :, pl.ds(w_start, WINDOW + tq)]
    w_back = tq_row - (w_start + lax.broadcasted_iota(jnp.int32, (WINDOW + tq, 1), 0))
    w_mask = (w_back >= 0) & (w_back < WINDOW)
    w_ss = [jnp.where(w_mask, _dot(kw_blk, qa_ref[0:dh, lanes]), NEG_INF) for lanes in head_lanes]
    w_ps = [jnp.exp2(s - jnp.max(s, axis=0, keepdims=True)) for s in w_ss]
    w_os = [_dot(vw_t, p.astype(BF16)) * (1.0 / jnp.sum(p, axis=0, keepdims=True)) for p in w_ps]
    for lanes, o_win in zip(head_lanes, w_os):
        owin_ref[:, lanes] = o_win

    c_end = lax.broadcasted_iota(jnp.int32, (n_cmp, 1), 0) * CMP_STRIDE + (CMP_BLOCK - 1)
    cmask = c_end <= tq_row
    has_cmp = tq_row >= CMP_BLOCK - 1
    scs = [jnp.where(cmask, _dot(kc_ref[0], qa_ref[0:dh, lanes]), NEG_INF) for lanes in head_lanes]
    es = [jnp.exp2(sc - jnp.max(sc, axis=0, keepdims=True)) for sc in scs]
    p_cmps = [e * jnp.where(has_cmp, 1.0 / jnp.sum(e, axis=0, keepdims=True), 0.0) for e in es]
    o_cmps = [_dot(vct_ref[0], p_cmp.astype(BF16)) for p_cmp in p_cmps]
    for lanes, o_cmp in zip(head_lanes, o_cmps):
        ocmp_ref[:, lanes] = o_cmp
    p_sum = functools.reduce(jnp.add, p_cmps)
    ovt = ovt_ref[...]
    imp_t = sum(_dot(ovt, part) for part in _split3(p_sum))
    n_blk = imp_t.shape[0]
    blk = lax.broadcasted_iota(jnp.int32, (n_blk, 1), 0)
    cur = tq_row // SLC_BLOCK
    forced = (blk == 0) | (blk == cur) | (blk == cur - 1)
    score_ref[...] = jnp.where(forced, FORCE_SCORE, jnp.where(blk * SLC_BLOCK <= tq_row, imp_t, -1.0))
    rank_ref[...] = jnp.zeros_like(rank_ref)
    sub = 8
    last_blk = (q0 + tq - 1) // SLC_BLOCK
    for kg in range(0, n_blk, sub):
        @pl.when(kg <= last_blk)
        def _(kg=kg):
            for ng in range(0, n_blk, sub):
                sn = score_ref[ng:ng + sub, :]
                rn = rank_ref[ng:ng + sub, :]
                for k in range(kg, kg + sub):
                    sk = score_ref[k:k + 1, :]
                    if ng > kg:
                        ahead = sk >= sn
                    elif ng < kg:
                        ahead = sk > sn
                    else:
                        later = lax.broadcasted_iota(jnp.int32, (sub, 1), 0) > (k - kg)
                        ahead = (sk > sn) | (later & (sk == sn))
                    rn = jnp.where(ahead, rn + 1.0, rn)
                rank_ref[ng:ng + sub, :] = rn
    sel_bias = jnp.where(rank_ref[...] < SLC_TOPN, 0.0, NEG_INF).astype(BF16)
    for lanes in head_lanes:
        qa_ref[dh:2 * dh, lanes] = sel_bias

    def key_block(k_ref, vt_ref, kdim, k0, n_keys, window, init):
        start = pl.multiple_of(k0, tq)
        k_blk = k_ref[0, pl.ds(start, n_keys), :]
        vt = vt_ref[0, :, pl.ds(start, n_keys)]
        ss = [_dot(k_blk, qa_ref[0:kdim, lanes]) for lanes in head_lanes]
        if window is not None:
            back = tq_row - (start + lax.broadcasted_iota(jnp.int32, (n_keys, 1), 0))
            mask = back >= 0
            if window != float("inf"):
                mask = mask & (back < window)
            ss = [jnp.where(mask, s, NEG_INF) for s in ss]
        mxs = [jnp.max(s, axis=0, keepdims=True) for s in ss]
        if init:
            ps = [jnp.exp2(s - mx) for s, mx in zip(ss, mxs)]
            for lanes, mx, p in zip(head_lanes, mxs, ps):
                m_ref[:, lanes] = mx
                l_ref[:, lanes] = jnp.sum(p, axis=0, keepdims=True)
                acc_ref[:, lanes] = _dot(vt, p.astype(BF16))
        else:
            m_prevs = [m_ref[:, lanes] for lanes in head_lanes]
            m_news = [jnp.maximum(mp, mx) for mp, mx in zip(m_prevs, mxs)]
            alphas = [jnp.exp2(mp - mn) for mp, mn in zip(m_prevs, m_news)]
            ps = [jnp.exp2(s - mn) for s, mn in zip(ss, m_news)]
            pvs = [_dot(vt, p.astype(BF16)) for p in ps]
            for lanes, mn, al, p, pv in zip(head_lanes, m_news, alphas, ps, pvs):
                m_ref[:, lanes] = mn
                l_ref[:, lanes] = al * l_ref[:, lanes] + jnp.sum(p, axis=0, keepdims=True)
                acc_ref[:, lanes] = al * acc_ref[:, lanes] + pv

    for rem in range(pair):
        @pl.when(i % pair == rem)
        def _(rem=rem):
            key_block(ksa_ref, vst_ref, 2 * dh, (i - rem) * tq, (rem + 1) * tq, float("inf"), True)

    def slc_body(kt, carry):
        key_block(ksa_ref, vst_ref, 2 * dh, kt * (pair * tq), pair * tq, None, False)
        return carry

    lax.fori_loop(0, i // pair, slc_body, 0)
    for h, lanes in enumerate(head_lanes):
        ocmp_ref[:, lanes] = (gt[3 * h:3 * h + 1, :] * ocmp_ref[:, lanes]
                              + gt[3 * h + 1:3 * h + 2, :] * (acc_ref[:, lanes] / l_ref[:, lanes]))

    outs = [ocmp_ref[:, lanes] + gt[3 * h + 2:3 * h + 3, :] * owin_ref[:, lanes]
            for h, lanes in enumerate(head_lanes)]
    o_ref[...] = jnp.concatenate(outs, axis=0).T.astype(o_ref.dtype)


def _attn(proj, small, kc, vct, ksa, vst, kwn, vwt, q_gain_col, ovt, *, batch, groups, tq):
    n = proj.shape[0]
    s = n // batch
    nq = s // tq
    r, dh = NSA_GROUP, NSA_HEAD_DIM
    m_cols = r * tq
    q_col = _P_Q // (r * dh)
    assert q_col * r * dh == _P_Q and s % tq == 0
    assert WINDOW % tq == 0 and s >= WINDOW + tq and s // SLC_BLOCK <= dh and tq % SLC_BLOCK == 0
    assert groups * _GATE_SLOT <= _SMALL_WIDTH and 3 * r <= _GATE_SLOT
    n_cmp = kc.shape[1]
    per_bg = lambda shape: pl.BlockSpec((1,) + shape, lambda bg, i: (bg, 0, 0))
    return pl.pallas_call(
        functools.partial(_attn_kernel, tq=tq, pair=_ATTN_PAIR, groups=groups),
        grid=(batch * groups, nq),
        in_specs=[
            pl.BlockSpec((tq, r * dh), lambda bg, i: ((bg // groups) * nq + i, q_col + bg % groups)),
            pl.BlockSpec((tq, _SMALL_WIDTH), lambda bg, i: ((bg // groups) * nq + i, 0)),
            per_bg((n_cmp, dh)), per_bg((dh, n_cmp)),
            per_bg((s, 2 * dh)), per_bg((dh, s)), per_bg((s, dh)), per_bg((dh, s)),
            pl.BlockSpec((dh, 1), lambda bg, i: (0, 0)),
            pl.BlockSpec(ovt.shape, lambda bg, i: (0, 0)),
        ],
        out_specs=pl.BlockSpec((tq, r * dh), lambda bg, i: ((bg // groups) * nq + i, bg % groups)),
        out_shape=jax.ShapeDtypeStruct((n, groups * r * dh), BF16),
        scratch_shapes=[pltpu.VMEM((2 * dh, m_cols), BF16),
                        pltpu.VMEM((1, m_cols), F32), pltpu.VMEM((1, m_cols), F32),
                        pltpu.VMEM((dh, m_cols), F32), pltpu.VMEM((dh, m_cols), F32),
                        pltpu.VMEM((dh, m_cols), F32),
                        pltpu.VMEM((dh, tq), F32), pltpu.VMEM((dh, tq), F32),
                        pltpu.VMEM((_SMALL_WIDTH, tq), F32)],
        compiler_params=_params("parallel", "parallel"),
        name="nsa_attn",
    )(proj, small, kc, vct, ksa, vst, kwn, vwt, q_gain_col, ovt)


def _mlstm_kernel(mq_ref, mk_ref, mv_ref, mo_ref, small_ref, gt_ref, cw_ref, cb_ref, og_ref, cum_ref,
                  o_ref, xbuf_ref, c_ref, n_ref, m_ref, *, tt):
    hd = M_HEAD_DIM
    L = M_CHUNK
    width = M_WIDTH
    n_chunk = tt // L
    j = pl.program_id(1)

    @pl.when(j == 0)
    def _():
        xbuf_ref[0:8, :] = jnp.zeros((8, 2 * width), F32)
        c_ref[...] = jnp.zeros_like(c_ref)
        n_ref[...] = jnp.zeros_like(n_ref)
        m_ref[...] = jnp.zeros_like(m_ref)

    @pl.when(j > 0)
    def _():
        xbuf_ref[0:8, :] = xbuf_ref[tt:tt + 8, :]

    xbuf_ref[8:8 + tt, 0:width] = mq_ref[...].astype(F32)
    xbuf_ref[8:8 + tt, width:2 * width] = mk_ref[...].astype(F32)
    conv = cb_ref[...]
    for tap in range(CONV_WIDTH):
        off = 8 - (CONV_WIDTH - 1) + tap
        conv = conv + cw_ref[tap:tap + 1, :] * xbuf_ref[off:off + tt, :]
    qk = conv * jax.nn.sigmoid(conv)
    q_all = qk[:, 0:width]
    k_all = qk[:, width:2 * width] * (hd ** -0.5)

    small = small_ref[...]
    logf_col = jax.nn.log_sigmoid(small)
    gt = gt_ref[0].reshape(n_chunk * 8, L)
    logf_row = jax.nn.log_sigmoid(gt)
    cum_u = cum_ref[0]
    cum_l = cum_ref[1]
    a_row_all = sum(_dot(part, cum_u) for part in _split3(logf_row))
    causal = lax.broadcasted_iota(jnp.int32, (L, L), 1) <= lax.broadcasted_iota(jnp.int32, (L, L), 0)

    for c in range(n_chunk):
        rows = slice(c * L, (c + 1) * L)
        a_col_c = sum(_dot(cum_l, part) for part in _split3(logf_col[rows]))
        pending = []
        for h in range(M_HEADS):
            lanes = slice(h * hd, (h + 1) * hd)
            q_c = q_all[rows, lanes]
            k_c = k_all[rows, lanes]
            v_c = mv_ref[rows, lanes]
            q_b = q_c.astype(BF16)
            a_j = a_col_c[:, _S_MF + h:_S_MF + h + 1]
            li_j = small[rows, _S_MI + h:_S_MI + h + 1]
            a_s = a_row_all[c * 8 + M_HEADS + h:c * 8 + M_HEADS + h + 1, :]
            li_s = gt[c * 8 + h:c * 8 + h + 1, :]
            g_c = a_j[L - 1:L, :]
            m_st = m_ref[h:h + 1, 0:1]
            c_st = c_ref[h]
            n_st = n_ref[h]

            log_w = jnp.where(causal, a_j - a_s + li_s, NEG_INF)
            m_intra = jnp.max(log_w, axis=-1, keepdims=True)
            w = jnp.exp(log_w - m_intra) * lax.dot_general(q_b, k_c.astype(BF16), _NT, preferred_element_type=F32)
            num_intra = _dot(w.astype(BF16), v_c)
            den_intra = jnp.sum(w, axis=-1, keepdims=True)

            num_inter = _dot(q_b, c_st.astype(BF16))
            den_inter = jnp.sum(q_c * n_st, axis=-1, keepdims=True)

            log_inter = a_j + m_st
            m_comb = jnp.maximum(log_inter, m_intra)
            s_inter = jnp.exp(log_inter - m_comb)
            s_intra = jnp.exp(m_intra - m_comb)
            num = s_inter * num_inter + s_intra * num_intra
            den = s_inter * den_inter + s_intra * den_intra
            hcell = num / jnp.maximum(jnp.abs(den), jnp.exp(-m_comb))

            log_u = g_c - a_j + li_j
            m_new = jnp.maximum(g_c + m_st, jnp.max(log_u, axis=0, keepdims=True))
            decay = jnp.exp(g_c + m_st - m_new)
            uk = jnp.exp(log_u - m_new) * k_c
            c_new = decay * c_st + lax.dot_general(uk.astype(BF16), v_c, _TN, preferred_element_type=F32)
            n_new = decay * n_st + jnp.sum(uk, axis=0, keepdims=True)

            hn = _rms(hcell, og_ref[:, lanes])
            o_gate = jax.nn.sigmoid(mo_ref[rows, lanes].astype(F32))
            pending.append((h, lanes, c_new, n_new, m_new, (o_gate * hn).astype(o_ref.dtype)))
        for h, lanes, c_new, n_new, m_new, out in pending:
            c_ref[h] = c_new
            n_ref[h] = n_new
            m_ref[h:h + 1, :] = jnp.broadcast_to(m_new, (1, m_ref.shape[1]))
            o_ref[rows, lanes] = out


def _mlstm(proj, small, gate_t, conv_w, conv_b, out_gain, cum_u, *, batch, tt=256):
    n = proj.shape[0]
    s = n // batch
    assert s % tt == 0 and tt % M_CHUNK == 0
    nt = s // tt
    cq = _P_MQ // M_WIDTH
    assert cq * M_WIDTH == _P_MQ
    col = lambda k: pl.BlockSpec((tt, M_WIDTH), lambda b, j: (b * nt + j, k))
    full = lambda shape: pl.BlockSpec(shape, lambda b, j: (0,) * len(shape))
    return pl.pallas_call(
        functools.partial(_mlstm_kernel, tt=tt),
        grid=(batch, nt),
        in_specs=[col(cq), col(cq + 1), col(cq + 2), col(cq + 3),
                  pl.BlockSpec((tt, _SMALL_WIDTH), lambda b, j: (b * nt + j, 0)),
                  pl.BlockSpec((1, tt // M_CHUNK, 8, M_CHUNK), lambda b, j: (b, j, 0, 0)),
                  full(conv_w.shape), full(conv_b.shape), full(out_gain.shape), full(cum_u.shape)],
        out_specs=pl.BlockSpec((tt, M_WIDTH), lambda b, j: (b * nt + j, 0)),
        out_shape=jax.ShapeDtypeStruct((n, M_WIDTH), BF16),
        scratch_shapes=[pltpu.VMEM((tt + 8, 2 * M_WIDTH), F32),
                        pltpu.VMEM((M_HEADS, M_HEAD_DIM, M_HEAD_DIM), F32),
                        pltpu.VMEM((M_HEADS, 1, M_HEAD_DIM), F32),
                        pltpu.VMEM((8, 128), F32)],
        compiler_params=_params("parallel", "arbitrary"),
        name="mlstm",
    )(proj, proj, proj, proj, small, gate_t, conv_w, conv_b, out_gain, cum_u)


def _merge_kernel(x_ref, oa_ref, ob_ref, ga_ref, gb_ref, wa_ref, wb_ref, wo_ref, o_ref):
    ya = _dot(oa_ref[...], wa_ref[...])
    yb = _dot(ob_ref[...], wb_ref[...])
    merged = (jax.nn.sigmoid(ga_ref[...].astype(F32)) * ya
              + jax.nn.sigmoid(gb_ref[...].astype(F32)) * yb)
    o_ref[...] = x_ref[...] + _dot(merged.astype(BF16), wo_ref[...])


def _merge(x, o_nsa, h_m, proj, wa, wb, wo, *, tm=256):
    n, d = x.shape
    assert n % tm == 0 and _P_MERGE == 0
    const = lambda shape: pl.BlockSpec(shape, lambda i: (0, 0), pipeline_mode=pl.Buffered(1))
    return pl.pallas_call(
        _merge_kernel,
        grid=(n // tm,),
        in_specs=[
            pl.BlockSpec((tm, d), lambda i: (i, 0)),
            pl.BlockSpec((tm, o_nsa.shape[1]), lambda i: (i, 0)),
            pl.BlockSpec((tm, h_m.shape[1]), lambda i: (i, 0)),
            pl.BlockSpec((tm, d), lambda i: (i, 0)),
            pl.BlockSpec((tm, d), lambda i: (i, 1)),
            const(wa.shape), const(wb.shape), const(wo.shape),
        ],
        out_specs=pl.BlockSpec((tm, d), lambda i: (i, 0)),
        out_shape=jax.ShapeDtypeStruct((n, d), F32),
        compiler_params=_params("parallel"),
        name="merge",
    )(x, o_nsa, h_m, proj, proj, wa, wb, wo)


def _layer(x2, b, s, p):
    n = x2.shape[0]
    g, r, dh = NSA_KV_HEADS, NSA_GROUP, NSA_HEAD_DIM
    w_in, b_in = p["w_in"], p["b_in"]

    def cols(a, start, size):
        return a[..., start:start + size]

    def kv_group_major(a):
        kv = cols(a, _OFF_KV, 6 * NSA_KV_WIDTH)
        kv = kv.reshape(a.shape[:-1] + (6, g, dh))
        return jnp.swapaxes(kv, -3, -2).reshape(a.shape[:-1] + (6 * NSA_KV_WIDTH,))

    def gate_slots(a):
        gates = cols(a, _OFF_G, 3 * NSA_HEADS).reshape(a.shape[:-1] + (g, 3 * r))
        pad = [(0, 0)] * (gates.ndim - 1) + [(0, _GATE_SLOT - 3 * r)]
        return jnp.pad(gates, pad).reshape(a.shape[:-1] + (g * _GATE_SLOT,))

    def main_cols(a):
        return jnp.concatenate([cols(a, _OFF_MERGE, 2 * D_MODEL), cols(a, _OFF_Q, NSA_Q_WIDTH),
                                cols(a, _OFF_MQKV, 3 * M_WIDTH), cols(a, _OFF_MO, M_WIDTH),
                                kv_group_major(a)], axis=-1)

    def small_cols(a):
        used = jnp.concatenate([gate_slots(a), cols(a, _OFF_MI, M_HEADS), cols(a, _OFF_MF, M_HEADS)], axis=-1)
        pad = [(0, 0)] * (used.ndim - 1) + [(0, _SMALL_WIDTH - used.shape[-1])]
        return jnp.pad(used, pad)

    w_main = main_cols(w_in).astype(BF16)
    b_main = main_cols(b_in)[None, :]
    w_small = small_cols(w_in).astype(BF16)
    b_small = small_cols(b_in)[None, :]

    x1, hn, small = _ffn(x2, p["ffn1_norm"][None, :], p["ffn1_w_gate"].astype(BF16),
                         p["ffn1_w_up"].astype(BF16), p["ffn1_w_down"].astype(BF16),
                         mix=(p["mix_norm"][None, :], w_small, b_small))
    proj = _proj(hn, w_main, b_main)

    rows = s // CMP_STRIDE
    hid = p["cmp_w1_k"].shape[1]
    w1k4 = p["cmp_w1_k"].reshape(2, CMP_STRIDE, dh, hid)
    w1v4 = p["cmp_w1_v"].reshape(2, CMP_STRIDE, dh, hid)
    zeros = jnp.zeros((CMP_STRIDE, dh, 2 * hid), F32)
    wcat = jnp.concatenate([
        jnp.concatenate([w1k4[0], w1k4[1], zeros], axis=-1),
        jnp.concatenate([zeros, w1v4[0], w1v4[1]], axis=-1)], axis=1).astype(BF16)
    kc, vct, ksa, vst, kwn, vwt = _prep(
        proj, wcat,
        p["cmp_pos_k"].reshape(1, CMP_BLOCK * dh), p["cmp_w1_k"].astype(BF16), p["cmp_w2_k"].astype(BF16),
        p["cmp_pos_v"].reshape(1, CMP_BLOCK * dh), p["cmp_w1_v"].astype(BF16), p["cmp_w2_v"].T.astype(BF16),
        p["nsa_kc_gain"][None, :], p["nsa_ks_gain"][None, :], p["nsa_kw_gain"][None, :], batch=b, groups=g)
    c_start = np.arange(rows) * CMP_STRIDE
    b_start = np.arange(dh) * SLC_BLOCK
    ovt = ((c_start[None, :] < (b_start + SLC_BLOCK)[:, None]) & ((c_start + CMP_BLOCK)[None, :] > b_start[:, None])
           & (np.arange(rows) < rows - 1)[None, :])
    o_nsa = _attn(proj, small, kc, vct, ksa, vst, kwn, vwt, p["nsa_q_gain"][:, None],
                  jnp.asarray(ovt, BF16), batch=b, groups=g, tq=_ATTN_TQ)

    gate_t = small[:, _S_MI:_S_MI + 2 * M_HEADS].reshape(b, s // M_CHUNK, M_CHUNK, 2 * M_HEADS)
    gate_t = gate_t.transpose(0, 1, 3, 2)
    upper = np.triu(np.ones((M_CHUNK, M_CHUNK), np.float32))
    cum_u = jnp.asarray(np.stack([upper, upper.T]), BF16)
    h_m = _mlstm(proj, small, gate_t, p["m_conv_w"], p["m_conv_b"][None, :],
                 p["m_out_gain"].reshape(1, M_WIDTH), cum_u, batch=b)

    x3 = _merge(x1, o_nsa, h_m, proj, p["w_branch_nsa"].astype(BF16), p["w_branch_mlstm"].astype(BF16),
                p["w_out"].astype(BF16))
    return _ffn(x3, p["ffn2_norm"][None, :], p["ffn2_w_gate"].astype(BF16),
                p["ffn2_w_up"].astype(BF16), p["ffn2_w_down"].astype(BF16))


def kernel(x, ffn1_norm, ffn1_w_gate, ffn1_w_up, ffn1_w_down, mix_norm, w_in, b_in, nsa_q_gain, nsa_kc_gain, nsa_ks_gain, nsa_kw_gain, cmp_pos_k, cmp_w1_k, cmp_w2_k, cmp_pos_v, cmp_w1_v, cmp_w2_v, m_conv_w, m_conv_b, m_out_gain, w_branch_nsa, w_branch_mlstm, w_out, ffn2_norm, ffn2_w_gate, ffn2_w_up, ffn2_w_down):
    params = dict(ffn1_norm=ffn1_norm, ffn1_w_gate=ffn1_w_gate, ffn1_w_up=ffn1_w_up, ffn1_w_down=ffn1_w_down,
                  mix_norm=mix_norm, w_in=w_in, b_in=b_in, nsa_q_gain=nsa_q_gain, nsa_kc_gain=nsa_kc_gain,
                  nsa_ks_gain=nsa_ks_gain, nsa_kw_gain=nsa_kw_gain, cmp_pos_k=cmp_pos_k, cmp_w1_k=cmp_w1_k,
                  cmp_w2_k=cmp_w2_k, cmp_pos_v=cmp_pos_v, cmp_w1_v=cmp_w1_v, cmp_w2_v=cmp_w2_v,
                  m_conv_w=m_conv_w, m_conv_b=m_conv_b, m_out_gain=m_out_gain, w_branch_nsa=w_branch_nsa,
                  w_branch_mlstm=w_branch_mlstm, w_out=w_out, ffn2_norm=ffn2_norm, ffn2_w_gate=ffn2_w_gate,
                  ffn2_w_up=ffn2_w_up, ffn2_w_down=ffn2_w_down)
    b, s, d = x.shape
    h = x.reshape(b * s, d)
    for l in range(ffn1_norm.shape[0]):
        h = _layer(h, b, s, {k: v[l] for k, v in params.items()})
    return h.reshape(b, s, d)
```
